```python
import math
import jax, jax.numpy as jnp
from jax import lax
import numpy as np

D_MODEL = 1024
BATCH = 32
SEQ = 2048
DEPTH = 1
DEC_BATCH = 32
DEC_SEQ = 32
PAST_LEN = 1024

CHUNK = 64
EPS = 1e-6
A_HEADS = 4
A_DK = 64
A_DV = 128
A_QBLOCK = 128
R_HEADS = 4
R_DK = 128
R_DV = 128
R_BLOCK = 16
P_HEADS = 8
P_NKEYS = 128
P_NEXPERTS = P_NKEYS * P_NKEYS
P_DQ = 128
P_TOPK = 16
P_TBLOCK = 256

A_QK_W = A_HEADS * 2 * A_DK
A_V_W = A_HEADS * A_DV
R_K_W = R_HEADS * R_DK
R_V_W = R_HEADS * R_DV
IN_SPLITS = (A_QK_W, 2 * A_QK_W, 2 * A_QK_W + A_V_W, 2 * A_QK_W + A_V_W + R_K_W,
             2 * A_QK_W + A_V_W + 2 * R_K_W, 2 * A_QK_W + A_V_W + 2 * R_K_W + R_V_W,
             2 * A_QK_W + A_V_W + 2 * R_K_W + 2 * R_V_W)
IN_COLS = 2 * A_QK_W + A_V_W + 2 * R_K_W + 2 * R_V_W + 2 * D_MODEL

kernel_name = 'diffattn_hgrn2_peer_streaming_step'


def rmsnorm(x, w):
    xf = x.astype(jnp.float32)
    y = xf * lax.rsqrt(jnp.mean(xf * xf, axis=-1, keepdims=True) + EPS)
    return (y * w.astype(jnp.float32)).astype(x.dtype)


def head_rms(o, w):
    return o * lax.rsqrt(jnp.mean(o * o, axis=-1, keepdims=True) + EPS) * w.astype(jnp.float32)


def chunk_mask(q_pos, k_pos):
    return (k_pos[None, :] // CHUNK) <= (q_pos[:, None] // CHUNK)


def diff_softmax_mix(q, k, v, mask, lam):
    s = jnp.einsum('bqhmd,bkhmd->bhmqk', q.astype(jnp.float32), k.astype(jnp.float32)) * (A_DK ** -0.5)
    s = jnp.where(mask, s, -jnp.inf)
    p = jax.nn.softmax(s, axis=-1)
    w = p[:, :, 0] - lam * p[:, :, 1]
    return jnp.einsum('bhqk,bkhv->bqhv', w, v.astype(jnp.float32))


def diff_attn_prompt(q, k, v, lam):
    B, T = q.shape[0], q.shape[1]
    nb = T // A_QBLOCK
    qb = q.reshape(B, nb, A_QBLOCK, A_HEADS, 2, A_DK).transpose(1, 0, 2, 3, 4, 5)
    k_pos = jnp.arange(T)

    def blk(args):
        q_blk, j = args
        q_pos = j * A_QBLOCK + jnp.arange(A_QBLOCK)
        return diff_softmax_mix(q_blk, k, v, chunk_mask(q_pos, k_pos), lam)

    o = lax.map(blk, (qb, jnp.arange(nb)))
    return o.transpose(1, 0, 2, 3, 4).reshape(B, T, A_HEADS, A_DV)


def gla_chunked(q, k, v, log_f, s0):
    B, T, H, DK = q.shape
    DV = v.shape[-1]
    pad = (-T) % R_BLOCK
    if pad:
        pw = ((0, 0), (0, pad), (0, 0), (0, 0))
        q, k, v, log_f = (jnp.pad(a, pw) for a in (q, k, v, log_f))
    n = (T + pad) // R_BLOCK
    q, k, v, log_f = (a.reshape(B, n, R_BLOCK, H, a.shape[-1]) for a in (q, k, v, log_f))
    b = jnp.cumsum(log_f, axis=2)
    b_last = b[:, :, -1:]
    q_in = q * jnp.exp(b)
    k_in = k * jnp.exp(-b)
    k_out = k * jnp.exp(b_last - b)
    causal = jnp.tril(jnp.ones((R_BLOCK, R_BLOCK), dtype=bool))
    a = jnp.where(causal, jnp.einsum('bnthd,bnshd->bnhts', q_in, k_in), 0.0)
    o_intra = jnp.einsum('bnhts,bnshv->bnthv', a, v)

    def step(S, xs):
        qi, ko, vi, dl = xs
        o = jnp.einsum('bthd,bhdv->bthv', qi, S)
        S = dl[..., None] * S + jnp.einsum('bshd,bshv->bhdv', ko, vi)
        return S, o

    xs = (q_in.swapaxes(0, 1), k_out.swapaxes(0, 1), v.swapaxes(0, 1),
          jnp.exp(b_last[:, :, 0]).swapaxes(0, 1))
    S, o_inter = lax.scan(step, s0, xs)
    o = o_intra + o_inter.swapaxes(0, 1)
    return o.reshape(B, n * R_BLOCK, H, DV)[:, :T], S


def token_mixer(xn, past_k, past_v, s0, lam, lam_init, w_in, a_subln, lb, r_gnorm, w_a, w_b, w_out):
    B, T, _ = xn.shape
    f32 = jnp.float32
    z = xn @ w_in
    qa, ka, va, f_pre, q_pre, i_pre, og_pre, gates = jnp.split(z, IN_SPLITS, axis=-1)
    qa = qa.reshape(B, T, A_HEADS, 2, A_DK)
    ka = ka.reshape(B, T, A_HEADS, 2, A_DK)
    va = va.reshape(B, T, A_HEADS, A_DV)
    if past_k is None:
        oa = diff_attn_prompt(qa, ka, va, lam)
    else:
        P = past_k.shape[1]
        k_all = jnp.concatenate([past_k.reshape(B, P, A_HEADS, 2, A_DK).astype(ka.dtype), ka], axis=1)
        v_all = jnp.concatenate([past_v.astype(va.dtype), va], axis=1)
        mask = chunk_mask(P + jnp.arange(T), jnp.arange(P + T))
        oa = diff_softmax_mix(qa, k_all, v_all, mask, lam)
    oa = head_rms(oa, a_subln) * (1.0 - lam_init)
    g = lb + (1.0 - lb) * jax.nn.sigmoid(f_pre.astype(f32))
    log_f = jnp.log(g).reshape(B, T, R_HEADS, R_DK)
    kr = (1.0 - g).reshape(B, T, R_HEADS, R_DK)
    qr = jax.nn.silu(q_pre.astype(f32)).reshape(B, T, R_HEADS, R_DK)
    ir = i_pre.astype(f32).reshape(B, T, R_HEADS, R_DV)
    orr, s_new = gla_chunked(qr, kr, ir, log_f, s0)
    orr = head_rms(orr, r_gnorm) * jax.nn.silu(og_pre.astype(f32)).reshape(B, T, R_HEADS, R_DV)
    dt = xn.dtype
    ga, gb = jnp.split(gates, 2, axis=-1)
    pa = oa.reshape(B, T, A_V_W).astype(dt) @ w_a
    pb = orr.reshape(B, T, R_V_W).astype(dt) @ w_b
    y = (jax.nn.sigmoid(ga) * pa + jax.nn.sigmoid(gb) * pb) @ w_out
    return y, ka.reshape(B, T, A_HEADS, 2 * A_DK), va, s_new


def peer(xn, w_q, sub_keys, down, up):
    B, T, D = xn.shape
    n = B * T
    xt = xn.reshape(n, D)
    pad = (-n) % P_TBLOCK
    if pad:
        xt = jnp.pad(xt, ((0, pad), (0, 0)))
    xb = xt.reshape(-1, P_TBLOCK, D)

    def blk(xs):
        q = (xs @ w_q).reshape(P_TBLOCK, P_HEADS, 2, P_DQ).astype(jnp.float32)
        s = jnp.einsum('thcd,hckd->thck', q, sub_keys.astype(jnp.float32))
        sv, si = lax.top_k(s, P_TOPK)
        comb = (sv[:, :, 0, :, None] + sv[:, :, 1, None, :]).reshape(P_TBLOCK, P_HEADS, P_TOPK * P_TOPK)
        cv, ci = lax.top_k(comb, P_TOPK)
        e = (jnp.take_along_axis(si[:, :, 0], ci // P_TOPK, axis=-1) * P_NKEYS
             + jnp.take_along_axis(si[:, :, 1], ci % P_TOPK, axis=-1))
        gate = jax.nn.softmax(cv, axis=-1)
        hid = jax.nn.gelu(jnp.einsum('thkd,td->thk', down[e], xs).astype(jnp.float32), approximate=False)
        return jnp.einsum('thk,thkd->td', (gate * hid).astype(xs.dtype), up[e])

    out = lax.map(blk, xb).reshape(-1, D)[:n]
    return out.reshape(B, T, D)


def trunk(x, past_k, past_v, past_s, lbs, norm1, w_in, lam_params, a_subln, r_gnorm,
          w_a, w_b, w_out, norm2, p_wq, p_keys, p_down, p_up, final_norm):
    B = x.shape[0]
    h = x
    ks, vs, ss = [], [], []
    for l in range(DEPTH):
        lam_init = 0.8 - 0.6 * math.exp(-0.3 * l)
        lp = lam_params[l].astype(jnp.float32)
        lam = jnp.exp(jnp.sum(lp[0] * lp[1])) - jnp.exp(jnp.sum(lp[2] * lp[3])) + lam_init
        if past_s is None:
            s0 = jnp.zeros((B, R_HEADS, R_DK, R_DV), jnp.float32)
            pk, pv = None, None
        else:
            s0 = past_s[l].astype(jnp.float32)
            pk, pv = past_k[l], past_v[l]
        y, nk, nv, ns = token_mixer(rmsnorm(h, norm1[l]), pk, pv, s0, lam, lam_init, w_in[l], a_subln[l],
                                    lbs[l], r_gnorm[l], w_a[l], w_b[l], w_out[l])
        h = h + y
        h = h + peer(rmsnorm(h, norm2[l]), p_wq[l], p_keys[l], p_down[l], p_up[l])
        ks.append(nk)
        vs.append(nv)
        ss.append(ns.astype(x.dtype))
    return rmsnorm(h, final_norm), jnp.stack(ks), jnp.stack(vs), jnp.stack(ss)


def setup_inputs(seed: int = 0) -> dict:
    key = jax.random.key(seed)
    ks = jax.random.split(key, 24)
    f32 = jnp.float32

    def nrm(k, shape, scale):
        return jax.random.normal(k, shape, f32) * scale

    def gain(k, shape):
        return 1.0 + 0.05 * jax.random.normal(k, shape, f32)

    return {
        'x_prompt': nrm(ks[0], (BATCH, SEQ, D_MODEL), 1.0),
        'x_sample': nrm(ks[1], (DEC_BATCH, DEC_SEQ, D_MODEL), 1.0),
        'cache_k': nrm(ks[2], (DEPTH, DEC_BATCH, PAST_LEN, A_HEADS, 2 * A_DK), 1.0),
        'cache_v': nrm(ks[3], (DEPTH, DEC_BATCH, PAST_LEN, A_HEADS, A_DV), 1.0),
        'state_hgrn': nrm(ks[4], (DEPTH, DEC_BATCH, R_HEADS, R_DK, R_DV), 0.5),
        'norm1': gain(ks[5], (DEPTH, D_MODEL)),
        'w_in': nrm(ks[6], (DEPTH, D_MODEL, IN_COLS), D_MODEL ** -0.5),
        'lam_params': nrm(ks[7], (DEPTH, 4, A_DK), 0.1),
        'a_subln': gain(ks[8], (DEPTH, A_DV)),
        'r_lb_logits': nrm(ks[9], (DEPTH + 1, R_K_W), 0.1),
        'r_gnorm': gain(ks[10], (DEPTH, R_DV)),
        'w_a': nrm(ks[11], (DEPTH, A_V_W, D_MODEL), A_V_W ** -0.5),
        'w_b': nrm(ks[12], (DEPTH, R_V_W, D_MODEL), R_V_W ** -0.5),
        'w_out': nrm(ks[13], (DEPTH, D_MODEL, D_MODEL), D_MODEL ** -0.5),
        'norm2': gain(ks[14], (DEPTH, D_MODEL)),
        'p_wq': nrm(ks[15], (DEPTH, D_MODEL, P_HEADS * 2 * P_DQ), D_MODEL ** -0.5),
        'p_keys': nrm(ks[16], (DEPTH, P_HEADS, 2, P_NKEYS, P_DQ), P_DQ ** -0.5),
        'p_down': nrm(ks[17], (DEPTH, P_NEXPERTS, D_MODEL), D_MODEL ** -0.5),
        'p_up': nrm(ks[18], (DEPTH, P_NEXPERTS, D_MODEL), P_HEADS ** -0.5),
        'final_norm': gain(ks[19], (D_MODEL,)),
    }


def reference(x_prompt, x_sample, cache_k, cache_v, state_hgrn, norm1, w_in, lam_params, a_subln,
              r_lb_logits, r_gnorm, w_a, w_b, w_out, norm2, p_wq, p_keys, p_down, p_up, final_norm):
    lbs = jnp.cumsum(jax.nn.softmax(r_lb_logits.astype(jnp.float32), axis=0), axis=0)[:DEPTH]
    y_prompt, k_prompt, v_prompt, s_prompt = trunk(
        x_prompt, None, None, None, lbs, norm1, w_in, lam_params, a_subln, r_gnorm,
        w_a, w_b, w_out, norm2, p_wq, p_keys, p_down, p_up, final_norm)
    y_sample, k_sample, v_sample, s_sample = trunk(
        x_sample, cache_k, cache_v, state_hgrn, lbs, norm1, w_in, lam_params, a_subln, r_gnorm,
        w_a, w_b, w_out, norm2, p_wq, p_keys, p_down, p_up, final_norm)
    return (y_prompt, y_sample, k_prompt, v_prompt, s_prompt, k_sample, v_sample, s_sample)
```

```python
import functools
import math

import jax
import jax.numpy as jnp
from jax import lax
from jax.experimental import pallas as pl
from jax.experimental.pallas import tpu as pltpu

F32 = jnp.float32
BF16 = jnp.bfloat16
EPS = 1e-6
CHUNK = 64
A_HEADS = 4
A_DK = 64
R_HEADS = 4
R_BLOCK = 16
P_HEADS = 8
P_NKEYS = 128
P_TOPK = 16
HK = P_HEADS * P_TOPK
HEAD_W = 128
N_IN_BLOCKS = 11
IN_BLOCK_W = 512
VMEM_LIMIT = 48 * 1024 * 1024


def _blk(n, pref):
    if n <= pref:
        return n
    b = pref
    while n % b:
        b //= 2
    assert b >= 8, (n, pref)
    return b


def _nt(a, b):
    return lax.dot_general(a, b, (((1,), (1,)), ((), ())), preferred_element_type=F32)


def _rms(x, gain):
    return x * lax.rsqrt(jnp.mean(x * x, axis=-1, keepdims=True) + EPS) * gain


def _inproj_kernel(x_ref, g_ref, w_ref, o_ref, xn_ref):
    @pl.when(pl.program_id(1) == 0)
    def _():
        xn_ref[...] = _rms(x_ref[...], g_ref[...]).astype(BF16)

    o_ref[...] = jnp.dot(xn_ref[...], w_ref[...], preferred_element_type=F32)


def _inproj(x, gain, w_bf16):
    n, d = x.shape
    tm = _blk(n, 512)
    return pl.pallas_call(
        _inproj_kernel,
        grid=(n // tm, N_IN_BLOCKS),
        in_specs=[
            pl.BlockSpec((tm, d), lambda i, j: (i, 0)),
            pl.BlockSpec((1, d), lambda i, j: (0, 0)),
            pl.BlockSpec((d, IN_BLOCK_W), lambda i, j: (0, j)),
        ],
        out_specs=pl.BlockSpec((None, tm, IN_BLOCK_W), lambda i, j: (j, i, 0)),
        out_shape=jax.ShapeDtypeStruct((N_IN_BLOCKS, n, IN_BLOCK_W), F32),
        scratch_shapes=[pltpu.VMEM((tm, d), BF16)],
        compiler_params=pltpu.CompilerParams(
            dimension_semantics=("parallel", "arbitrary"), vmem_limit_bytes=VMEM_LIMIT),
        name="inproj",
    )(x, gain, w_bf16)


def _lam_from_params(lp, lam_init):
    a = jnp.sum(lp[0:1] * lp[1:2], axis=(0, 1), keepdims=True)
    b = jnp.sum(lp[2:3] * lp[3:4], axis=(0, 1), keepdims=True)
    return jnp.exp(a) - jnp.exp(b) + lam_init


def _map_masks():
    lane = lax.broadcasted_iota(jnp.int32, (1, HEAD_W), 1)
    m1 = (lane < A_DK).astype(F32)
    return m1, 1.0 - m1


def _chunk_id(pos):
    return lax.shift_right_logical(pos, int(math.log2(CHUNK)))


def _attn_prompt_kernel(lam_ref, q_ref, k_ref, v_ref, sub_ref, o_ref, *, seq, qb, lam_init):
    lam = _lam_from_params(lam_ref[...], lam_init)
    m1, m2 = _map_masks()
    kb = k_ref[...].astype(BF16)
    vb = v_ref[...].astype(BF16)
    gain = sub_ref[...] * (1.0 - lam_init)
    for j in range(seq // qb):
        kv_len = (j + 1) * qb
        q = q_ref[j * qb:(j + 1) * qb, :] * (A_DK ** -0.5)
        kk = kb[0:kv_len]
        s1 = _nt((q * m1).astype(BF16), kk)
        s2 = _nt((q * m2).astype(BF16), kk)
        qpos = j * qb + lax.broadcasted_iota(jnp.int32, (qb, kv_len), 0)
        kpos = lax.broadcasted_iota(jnp.int32, (qb, kv_len), 1)
        mask = _chunk_id(kpos) <= _chunk_id(qpos)
        s1 = jnp.where(mask, s1, -jnp.inf)
        s2 = jnp.where(mask, s2, -jnp.inf)
        e1 = jnp.exp(s1 - jnp.max(s1, axis=-1, keepdims=True))
        e2 = jnp.exp(s2 - jnp.max(s2, axis=-1, keepdims=True))
        r1 = 1.0 / jnp.sum(e1, axis=-1, keepdims=True)
        r2 = lam / jnp.sum(e2, axis=-1, keepdims=True)
        w = (e1 * r1 - e2 * r2).astype(BF16)
        o = jnp.dot(w, vb[0:kv_len], preferred_element_type=F32)
        o_ref[j * qb:(j + 1) * qb, :] = _rms(o, gain)


def _attn_prompt(z4, lam_params, a_subln, lam_init):
    _, b, t, _ = z4.shape
    qb = _blk(t, 256)

    def zspec(col):
        return pl.BlockSpec((None, None, t, HEAD_W), lambda bi, h, col=col: (col, bi, 0, h))

    return pl.pallas_call(
        functools.partial(_attn_prompt_kernel, seq=t, qb=qb, lam_init=lam_init),
        grid=(b, A_HEADS),
        in_specs=[
            pl.BlockSpec((4, A_DK), lambda bi, h: (0, 0)),
            zspec(0), zspec(1), zspec(2),
            pl.BlockSpec((1, HEAD_W), lambda bi, h: (0, 0)),
        ],
        out_specs=pl.BlockSpec((None, t, HEAD_W), lambda bi, h: (bi, 0, h)),
        out_shape=jax.ShapeDtypeStruct((b, t, A_HEADS * HEAD_W), F32),
        compiler_params=pltpu.CompilerParams(
            dimension_semantics=("parallel", "parallel"), vmem_limit_bytes=VMEM_LIMIT),
        name="attn_prompt",
    )(lam_params, z4, z4, z4, a_subln)


def _attn_sample_kernel(lam_ref, q_ref, k_ref, v_ref, pk_ref, pv_ref, sub_ref, o_ref, *, seq, past, lam_init):
    lam = _lam_from_params(lam_ref[...], lam_init)
    m1, m2 = _map_masks()
    gain = sub_ref[...] * (1.0 - lam_init)
    q = q_ref[...] * (A_DK ** -0.5)
    kn = k_ref[...].astype(BF16)
    kp = pk_ref[...].astype(BF16)
    qpos_p = past + lax.broadcasted_iota(jnp.int32, (seq, past), 0)
    kpos_p = lax.broadcasted_iota(jnp.int32, (seq, past), 1)
    mask_p = _chunk_id(kpos_p) <= _chunk_id(qpos_p)
    qpos_n = past + lax.broadcasted_iota(jnp.int32, (seq, seq), 0)
    kpos_n = past + lax.broadcasted_iota(jnp.int32, (seq, seq), 1)
    mask_n = _chunk_id(kpos_n) <= _chunk_id(qpos_n)

    def one_map(qm):
        sp = jnp.where(mask_p, _nt(qm, kp), -jnp.inf)
        sn = jnp.where(mask_n, _nt(qm, kn), -jnp.inf)
        mx = jnp.maximum(jnp.max(sp, axis=-1, keepdims=True), jnp.max(sn, axis=-1, keepdims=True))
        ep = jnp.exp(sp - mx)
        en = jnp.exp(sn - mx)
        tot = jnp.sum(ep, axis=-1, keepdims=True) + jnp.sum(en, axis=-1, keepdims=True)
        return ep, en, tot

    ep1, en1, t1 = one_map((q * m1).astype(BF16))
    ep2, en2, t2 = one_map((q * m2).astype(BF16))
    r1 = 1.0 / t1
    r2 = lam / t2
    wp = (ep1 * r1 - ep2 * r2).astype(BF16)
    wn = (en1 * r1 - en2 * r2).astype(BF16)
    o = (jnp.dot(wp, pv_ref[...].astype(BF16), preferred_element_type=F32)
         + jnp.dot(wn, v_ref[...].astype(BF16), preferred_element_type=F32))
    o_ref[...] = _rms(o, gain)


def _attn_sample(z4, past_k, past_v, lam_params, a_subln, lam_init):
    _, b, t, _ = z4.shape
    p = past_k.shape[1]

    def zspec(col):
        return pl.BlockSpec((None, None, t, HEAD_W), lambda bi, h, col=col: (col, bi, 0, h))

    pspec = pl.BlockSpec((None, p, HEAD_W), lambda bi, h: (bi, 0, h))
    return pl.pallas_call(
        functools.partial(_attn_sample_kernel, seq=t, past=p, lam_init=lam_init),
        grid=(b, A_HEADS),
        in_specs=[
            pl.BlockSpec((4, A_DK), lambda bi, h: (0, 0)),
            zspec(0), zspec(1), zspec(2), pspec, pspec,
            pl.BlockSpec((1, HEAD_W), lambda bi, h: (0, 0)),
        ],
        out_specs=pl.BlockSpec((None, t, HEAD_W), lambda bi, h: (bi, 0, h)),
        out_shape=jax.ShapeDtypeStruct((b, t, A_HEADS * HEAD_W), F32),
        compiler_params=pltpu.CompilerParams(
            dimension_semantics=("parallel", "parallel"), vmem_limit_bytes=VMEM_LIMIT),
        name="attn_sample",
    )(lam_params, z4, z4, z4, past_k, past_v, a_subln)


def _hgrn_kernel(f_ref, q_ref, i_ref, og_ref, lbl_ref, gn_ref, s0_ref, o_ref, s_ref, st_ref, *, seq, ch):
    nb = ch // R_BLOCK
    shift = int(math.log2(R_BLOCK))
    logits = lbl_ref[...]
    ex = jnp.exp(logits - jnp.max(logits, axis=0, keepdims=True))
    lb = ex[0:1] / jnp.sum(ex, axis=0, keepdims=True)
    row = lax.broadcasted_iota(jnp.int32, (ch, ch), 0)
    col = lax.broadcasted_iota(jnp.int32, (ch, ch), 1)
    same = lax.shift_right_logical(row, shift) == lax.shift_right_logical(col, shift)
    causal = jnp.logical_and(same, col <= row)
    cum_m = causal.astype(F32)
    tot_m = same.astype(F32)
    gn = gn_ref[...]
    st_ref[...] = s0_ref[...].T

    def chunk(c, carry):
        sl = pl.ds(pl.multiple_of(c * ch, ch), ch)
        g = lb + (1.0 - lb) * jax.nn.sigmoid(f_ref[sl, :])
        logf = jnp.log(g)
        kk = 1.0 - g
        qp = q_ref[sl, :]
        qq = qp * jax.nn.sigmoid(qp)
        vv = i_ref[sl, :]
        b = jnp.dot(cum_m, logf, precision=lax.Precision.HIGHEST, preferred_element_type=F32)
        bl = jnp.dot(tot_m, logf, precision=lax.Precision.HIGHEST, preferred_element_type=F32)
        q_in = (qq * jnp.exp(b)).astype(BF16)
        k_in = (kk * jnp.exp(-b)).astype(BF16)
        k_out = (kk * jnp.exp(bl - b)).astype(BF16)
        vb = vv.astype(BF16)
        a = jnp.where(causal, _nt(q_in, k_in), 0.0)
        o_intra = jnp.dot(a.astype(BF16), vb, preferred_element_type=F32)
        st = st_ref[...]
        inter = []
        for blk in range(nb):
            r0 = blk * R_BLOCK
            inter.append(_nt(q_in[r0:r0 + R_BLOCK], st.astype(BF16)))
            dl = jnp.exp(bl[r0:r0 + 1])
            upd = lax.dot_general(vb[r0:r0 + R_BLOCK], k_out[r0:r0 + R_BLOCK],
                                  (((0,), (0,)), ((), ())), preferred_element_type=F32)
            st = st * dl + upd
        st_ref[...] = st
        o = o_intra + jnp.concatenate(inter, axis=0)
        ogp = og_ref[sl, :]
        o_ref[sl, :] = _rms(o, gn) * (ogp * jax.nn.sigmoid(ogp))
        return carry

    lax.fori_loop(0, seq // ch, chunk, 0)
    s_ref[...] = st_ref[...].T


def _hgrn(z4, r_lb_logits, r_gnorm, s0):
    _, b, t, _ = z4.shape
    assert t % R_BLOCK == 0
    ch = _blk(t, 128)
    nl = r_lb_logits.shape[0]

    def zspec(col):
        return pl.BlockSpec((None, None, t, HEAD_W), lambda bi, h, col=col: (col, bi, 0, h))

    sspec = pl.BlockSpec((None, None, HEAD_W, HEAD_W), lambda bi, h: (bi, h, 0, 0))
    return pl.pallas_call(
        functools.partial(_hgrn_kernel, seq=t, ch=ch),
        grid=(b, R_HEADS),
        in_specs=[
            zspec(3), zspec(4), zspec(5), zspec(6),
            pl.BlockSpec((nl, HEAD_W), lambda bi, h: (0, h)),
            pl.BlockSpec((1, HEAD_W), lambda bi, h: (0, 0)),
            sspec,
        ],
        out_specs=[pl.BlockSpec((None, t, HEAD_W), lambda bi, h: (bi, 0, h)), sspec],
        out_shape=[jax.ShapeDtypeStruct((b, t, R_HEADS * HEAD_W), F32),
                   jax.ShapeDtypeStruct((b, R_HEADS, HEAD_W, HEAD_W), F32)],
        scratch_shapes=[pltpu.VMEM((HEAD_W, HEAD_W), F32)],
        compiler_params=pltpu.CompilerParams(
            dimension_semantics=("parallel", "parallel"), vmem_limit_bytes=VMEM_LIMIT),
        name="hgrn2",
    )(z4, z4, z4, z4, r_lb_logits, r_gnorm, s0)


def _merge_kernel(oa_ref, or_ref, ga0_ref, ga1_ref, gb0_ref, gb1_ref, x_ref,
                  wa_ref, wb_ref, wo_ref, n2_ref, wq_ref, h_ref, xn_ref, q_ref):
    pa = jnp.dot(oa_ref[...].astype(BF16), wa_ref[...], preferred_element_type=F32)
    pb = jnp.dot(or_ref[...].astype(BF16), wb_ref[...], preferred_element_type=F32)
    ga = jnp.concatenate([ga0_ref[...], ga1_ref[...]], axis=-1)
    gb = jnp.concatenate([gb0_ref[...], gb1_ref[...]], axis=-1)
    m = jax.nn.sigmoid(ga) * pa + jax.nn.sigmoid(gb) * pb
    h = x_ref[...] + jnp.dot(m.astype(BF16), wo_ref[...], preferred_element_type=F32)
    h_ref[...] = h
    xn = _rms(h, n2_ref[...])
    xn_ref[...] = xn
    q_ref[...] = jnp.dot(xn.astype(BF16), wq_ref[...], preferred_element_type=F32)


def _merge(oa, orr, z, x, w_a, w_b, w_out, norm2, p_wq):
    n, d = x.shape
    tm = _blk(n, 256)
    qw = p_wq.shape[1]

    def zspec(col):
        return pl.BlockSpec((None, tm, IN_BLOCK_W), lambda i, col=col: (col, i, 0))

    def full(a):
        return pl.BlockSpec(a.shape, lambda i: (0,) * a.ndim)

    return pl.pallas_call(
        _merge_kernel,
        grid=(n // tm,),
        in_specs=[
            pl.BlockSpec((tm, oa.shape[1]), lambda i: (i, 0)),
            pl.BlockSpec((tm, orr.shape[1]), lambda i: (i, 0)),
            zspec(7), zspec(8), zspec(9), zspec(10),
            pl.BlockSpec((tm, d), lambda i: (i, 0)),
            full(w_a), full(w_b), full(w_out), full(norm2), full(p_wq),
        ],
        out_specs=[pl.BlockSpec((tm, d), lambda i: (i, 0)),
                   pl.BlockSpec((tm, d), lambda i: (i, 0)),
                   pl.BlockSpec((tm, qw), lambda i: (i, 0))],
        out_shape=[jax.ShapeDtypeStruct((n, d), F32),
                   jax.ShapeDtypeStruct((n, d), F32),
                   jax.ShapeDtypeStruct((n, qw), F32)],
        compiler_params=pltpu.CompilerParams(
            dimension_semantics=("parallel",), vmem_limit_bytes=VMEM_LIMIT),
        name="merge_proj",
    )(oa, orr, z, z, z, z, x, w_a, w_b, w_out, norm2, p_wq)


def _top16(s, n_rows):
    iota = lax.broadcasted_iota(jnp.int32, s.shape, 0)
    vals, idxs = [], []
    for _ in range(P_TOPK):
        m = jnp.max(s, axis=0, keepdims=True)
        am = jnp.min(jnp.where(s == m, iota, n_rows), axis=0, keepdims=True)
        vals.append(m)
        idxs.append(am)
        s = jnp.where(iota == am, -jnp.inf, s)
    return vals, idxs


def _retrieve_kernel(q_ref, keys_ref, e_ref, g_ref, *, tt):
    iota16 = lax.broadcasted_iota(jnp.int32, (P_TOPK, tt), 0)
    e_rows, g_rows = [], []
    for h in range(P_HEADS):
        sv, si = [], []
        for c in range(2):
            hc = h * 2 + c
            s = _nt(keys_ref[hc], q_ref[:, hc * P_NKEYS:(hc + 1) * P_NKEYS])
            vals, idxs = _top16(s, P_NKEYS)
            sv.append(vals)
            si.append(jnp.concatenate(idxs, axis=0))
        sv1 = jnp.concatenate(sv[1], axis=0)
        comb = jnp.concatenate([sv[0][i] + sv1 for i in range(P_TOPK)], axis=0)
        cvals, cidx = _top16(comb, P_TOPK * P_TOPK)
        for k in range(P_TOPK):
            ci = cidx[k]
            i0 = lax.shift_right_logical(ci, 4)
            i1 = jnp.bitwise_and(ci, P_TOPK - 1)
            e0 = jnp.sum(jnp.where(iota16 == i0, si[0], 0), axis=0, keepdims=True)
            e1 = jnp.sum(jnp.where(iota16 == i1, si[1], 0), axis=0, keepdims=True)
            e_rows.append(e0 * P_NKEYS + e1)
        cv = jnp.concatenate(cvals, axis=0)
        ex = jnp.exp(cv - cvals[0])
        g_rows.append(ex / jnp.sum(ex, axis=0, keepdims=True))
    e_all = jnp.concatenate(e_rows, axis=0)
    g_all = jnp.concatenate(g_rows, axis=0)
    e_ref[...] = (e_all * 4).astype(F32).T.astype(jnp.int32)
    g_ref[...] = g_all.T


def _retrieve(q, keys16):
    n, qw = q.shape
    tt = _blk(n, 128)
    return pl.pallas_call(
        functools.partial(_retrieve_kernel, tt=tt),
        grid=(n // tt,),
        in_specs=[pl.BlockSpec((tt, qw), lambda i: (i, 0)),
                  pl.BlockSpec(keys16.shape, lambda i: (0, 0, 0))],
        out_specs=[pl.BlockSpec((tt, HK), lambda i: (i, 0)),
                   pl.BlockSpec((tt, HK), lambda i: (i, 0))],
        out_shape=[jax.ShapeDtypeStruct((n, HK), jnp.int32),
                   jax.ShapeDtypeStruct((n, HK), F32)],
        compiler_params=pltpu.CompilerParams(
            dimension_semantics=("parallel",), vmem_limit_bytes=VMEM_LIMIT),
        name="peer_retrieve",
    )(q, keys16)


def _pack_table(tbl):
    u = lax.bitcast_convert_type(tbl.astype(BF16), jnp.uint16).astype(jnp.uint32)
    half = tbl.shape[1] // 2
    w = u[:, :half] | (u[:, half:] << 16)
    return lax.bitcast_convert_type(w, jnp.int32).reshape(tbl.shape[0] * 4, 128)


def _unpack(w):
    lo = lax.bitcast_convert_type(w << 16, F32)
    hi = lax.bitcast_convert_type(w & jnp.int32(-65536), F32)
    return lo, hi


def _gather_row(tbl_ref, e):
    return tbl_ref[pl.ds(pl.multiple_of(e, 4), 4), :]


def _down_kernel(idx_ref, x_ref, gate_ref, tbl_ref, sel_ref, c_ref, slab_ref, hid_ref, *, tb):
    ones8 = jnp.ones((8, 128), F32)

    def tok(t, carry):
        xt = x_ref[t]
        x_lo = xt[0:4]
        x_hi = xt[4:8]
        for hk in range(HK):
            lo, hi = _unpack(_gather_row(tbl_ref, idx_ref[t, hk]))
            slab_ref[hk * 4:(hk + 1) * 4, :] = lo * x_lo + hi * x_hi
        lanes = _nt(ones8, slab_ref[...])
        hid = jnp.dot(lanes, sel_ref[...], preferred_element_type=F32)
        hid_ref[pl.ds(t, 1), :] = hid[0:1]
        return carry

    lax.fori_loop(0, tb, tok, 0)
    hid = hid_ref[...]
    gelu = 0.5 * hid * (1.0 + lax.erf(hid * (2.0 ** -0.5)))
    c_ref[...] = gate_ref[...] * gelu


def _peer_down(idx, x3, gate, tbl_packed, sel):
    n = idx.shape[0]
    tb = _blk(n, 128)
    return pl.pallas_call(
        functools.partial(_down_kernel, tb=tb),
        grid=(n // tb,),
        in_specs=[
            pl.BlockSpec((tb, HK), lambda i: (i, 0), memory_space=pltpu.SMEM),
            pl.BlockSpec((tb, 8, 128), lambda i: (i, 0, 0)),
            pl.BlockSpec((tb, HK), lambda i: (i, 0)),
            pl.BlockSpec(memory_space=pltpu.VMEM),
            pl.BlockSpec(memory_space=pltpu.VMEM),
        ],
        out_specs=pl.BlockSpec((tb, HK), lambda i: (i, 0)),
        out_shape=jax.ShapeDtypeStruct((n, HK), F32),
        scratch_shapes=[pltpu.VMEM((HK * 4, 128), F32), pltpu.VMEM((tb, HK), F32)],
        compiler_params=pltpu.CompilerParams(
            dimension_semantics=("arbitrary",), vmem_limit_bytes=VMEM_LIMIT),
        name="peer_down",
    )(idx, x3, gate, tbl_packed, sel)


def _up_kernel(idx_ref, c_ref, tbl_ref, out_ref, *, tb):
    def tok(t, carry):
        accs = [jnp.zeros((4, 128), F32) for _ in range(4)]
        for hk in range(HK):
            lo, hi = _unpack(_gather_row(tbl_ref, idx_ref[t, hk]))
            c = c_ref[t, hk]
            j = (hk % 2) * 2
            accs[j] = accs[j] + c * lo
            accs[j + 1] = accs[j + 1] + c * hi
        out_ref[t, 0:4, :] = accs[0] + accs[2]
        out_ref[t, 4:8, :] = accs[1] + accs[3]
        return carry

    lax.fori_loop(0, tb, tok, 0)


def _peer_up(idx, c, tbl_packed):
    n = idx.shape[0]
    tb = _blk(n, 128)
    return pl.pallas_call(
        functools.partial(_up_kernel, tb=tb),
        grid=(n // tb,),
        in_specs=[
            pl.BlockSpec((tb, HK), lambda i: (i, 0), memory_space=pltpu.SMEM),
            pl.BlockSpec((tb, HK), lambda i: (i, 0), memory_space=pltpu.SMEM),
            pl.BlockSpec(memory_space=pltpu.VMEM),
        ],
        out_specs=pl.BlockSpec((tb, 8, 128), lambda i: (i, 0, 0)),
        out_shape=jax.ShapeDtypeStruct((n, 8, 128), F32),
        compiler_params=pltpu.CompilerParams(
            dimension_semantics=("arbitrary",), vmem_limit_bytes=VMEM_LIMIT),
        name="peer_up",
    )(idx, c, tbl_packed)


def _final_kernel(h_ref, p_ref, g_ref, y_ref):
    y_ref[...] = _rms(h_ref[...] + p_ref[...], g_ref[...])


def _final(h, peer, gain):
    n, d = h.shape
    tm = _blk(n, 512)
    return pl.pallas_call(
        _final_kernel,
        grid=(n // tm,),
        in_specs=[pl.BlockSpec((tm, d), lambda i: (i, 0)),
                  pl.BlockSpec((tm, d), lambda i: (i, 0)),
                  pl.BlockSpec((1, d), lambda i: (0, 0))],
        out_specs=pl.BlockSpec((tm, d), lambda i: (i, 0)),
        out_shape=jax.ShapeDtypeStruct((n, d), F32),
        compiler_params=pltpu.CompilerParams(
            dimension_semantics=("parallel",), vmem_limit_bytes=VMEM_LIMIT),
        name="final_norm",
    )(h, peer, gain)


def _trunk(x, past_k, past_v, past_s, prm):
    b, t, d = x.shape
    n = b * t
    lam_init = 0.8 - 0.6 * math.exp(-0.3 * 0)
    x2 = x.reshape(n, d)
    z = _inproj(x2, prm["norm1"], prm["w_in"])
    z4 = z.reshape(N_IN_BLOCKS, b, t, IN_BLOCK_W)
    if past_k is None:
        oa = _attn_prompt(z4, prm["lam"], prm["a_subln"], lam_init)
        s0 = jnp.zeros((b, R_HEADS, HEAD_W, HEAD_W), F32)
    else:
        p = past_k.shape[1]
        oa = _attn_sample(z4, past_k.reshape(b, p, A_HEADS * HEAD_W), past_v.reshape(b, p, A_HEADS * HEAD_W),
                          prm["lam"], prm["a_subln"], lam_init)
        s0 = past_s.astype(F32)
    orr, s_new = _hgrn(z4, prm["r_lb_logits"], prm["r_gnorm"], s0)
    h1, xn2, q = _merge(oa.reshape(n, -1), orr.reshape(n, -1), z, x2,
                        prm["w_a"], prm["w_b"], prm["w_out"], prm["norm2"], prm["p_wq"])
    eidx, gate = _retrieve(q, prm["p_keys"])
    c = _peer_down(eidx, xn2.reshape(n, 8, 128), gate, prm["down_packed"], prm["sel"])
    peer = _peer_up(eidx, c, prm["up_packed"]).reshape(n, d)
    y = _final(h1, peer, prm["final_norm"]).reshape(b, t, d)
    k_new = z[1].reshape(1, b, t, A_HEADS, HEAD_W)
    v_new = z[2].reshape(1, b, t, A_HEADS, HEAD_W)
    return y, k_new, v_new, s_new[None].astype(x.dtype)


def kernel(x_prompt, x_sample, cache_k, cache_v, state_hgrn, norm1, w_in, lam_params, a_subln, r_lb_logits, r_gnorm, w_a, w_b, w_out, norm2, p_wq, p_keys, p_down, p_up, final_norm):
    assert w_in.shape[0] == 1 and x_prompt.shape[-1] == 8 * 128
    sel = (lax.broadcasted_iota(jnp.int32, (HK * 4, HK), 0) // 4
           == lax.broadcasted_iota(jnp.int32, (HK * 4, HK), 1)).astype(F32)
    prm = {
        "norm1": norm1[0][None].astype(F32),
        "w_in": w_in[0].astype(BF16),
        "lam": lam_params[0].astype(F32),
        "a_subln": a_subln[0][None].astype(F32),
        "r_lb_logits": r_lb_logits.astype(F32),
        "r_gnorm": r_gnorm[0][None].astype(F32),
        "w_a": w_a[0].astype(BF16),
        "w_b": w_b[0].astype(BF16),
        "w_out": w_out[0].astype(BF16),
        "norm2": norm2[0][None].astype(F32),
        "p_wq": p_wq[0].astype(BF16),
        "p_keys": p_keys[0].reshape(P_HEADS * 2, P_NKEYS, -1).astype(F32),
        "down_packed": _pack_table(p_down[0]),
        "up_packed": _pack_table(p_up[0]),
        "sel": sel,
        "final_norm": final_norm[None].astype(F32),
    }
    y_p, k_p, v_p, s_p = _trunk(x_prompt, None, None, None, prm)
    y_s, k_s, v_s, s_s = _trunk(x_sample, cache_k[0], cache_v[0], state_hgrn[0], prm)
    return (y_p, y_s, k_p, v_p, s_p, k_s, v_s, s_s)
```

```python
import functools
import math

import jax
import jax.numpy as jnp
from jax import lax
from jax.experimental import pallas as pl
from jax.experimental.pallas import tpu as pltpu

F32 = jnp.float32
BF16 = jnp.bfloat16
EPS = 1e-6
CHUNK = 64
A_HEADS = 4
A_DK = 64
R_HEADS = 4
R_BLOCK = 16
P_HEADS = 8
P_NKEYS = 128
P_TOPK = 16
HK = P_HEADS * P_TOPK
HEAD_W = 128
N_IN_BLOCKS = 11
IN_BLOCK_W = 512
VMEM_LIMIT = 52 * 1024 * 1024


def _blk(n, pref):
    if n <= pref:
        return n
    b = pref
    while n % b:
        b //= 2
    assert b >= 8, (n, pref)
    return b


def _nt(a, b):
    return lax.dot_general(a, b, (((1,), (1,)), ((), ())), preferred_element_type=F32)


def _rms(x, gain):
    return x * lax.rsqrt(jnp.mean(x * x, axis=-1, keepdims=True) + EPS) * gain


def _inproj_kernel(x_ref, g_ref, w_ref, o_ref, xn_ref):
    @pl.when(pl.program_id(1) == 0)
    def _():
        xn_ref[...] = _rms(x_ref[...], g_ref[...]).astype(BF16)

    o_ref[...] = jnp.dot(xn_ref[...], w_ref[...], preferred_element_type=F32)


def _inproj(x, gain, w_bf16):
    n, d = x.shape
    tm = _blk(n, 2048)
    return pl.pallas_call(
        _inproj_kernel,
        grid=(n // tm, N_IN_BLOCKS),
        in_specs=[
            pl.BlockSpec((tm, d), lambda i, j: (i, 0)),
            pl.BlockSpec((1, d), lambda i, j: (0, 0)),
            pl.BlockSpec((d, IN_BLOCK_W), lambda i, j: (0, j)),
        ],
        out_specs=pl.BlockSpec((None, tm, IN_BLOCK_W), lambda i, j: (j, i, 0)),
        out_shape=jax.ShapeDtypeStruct((N_IN_BLOCKS, n, IN_BLOCK_W), F32),
        scratch_shapes=[pltpu.VMEM((tm, d), BF16)],
        compiler_params=pltpu.CompilerParams(
            dimension_semantics=("parallel", "arbitrary"), vmem_limit_bytes=VMEM_LIMIT),
        name="inproj",
    )(x, gain, w_bf16)


def _lam_from_params(lp, lam_init):
    a = jnp.sum(lp[0:1] * lp[1:2], axis=(0, 1), keepdims=True)
    b = jnp.sum(lp[2:3] * lp[3:4], axis=(0, 1), keepdims=True)
    return jnp.exp(a) - jnp.exp(b) + lam_init


def _map_masks():
    lane = lax.broadcasted_iota(jnp.int32, (1, HEAD_W), 1)
    m1 = (lane < A_DK).astype(F32)
    return m1, 1.0 - m1


def _chunk_id(pos):
    return lax.shift_right_logical(pos, int(math.log2(CHUNK)))


def _attn_prompt_kernel(lam_ref, q_ref, k_ref, v_ref, sub_ref, o_ref, *, seq, qb, lam_init):
    lam = _lam_from_params(lam_ref[...], lam_init)
    m1, m2 = _map_masks()
    kb = k_ref[...].astype(BF16)
    vb = v_ref[...].astype(BF16)
    gain = sub_ref[...] * (1.0 - lam_init)
    for j in range(seq // qb):
        kv_len = (j + 1) * qb
        q = q_ref[j * qb:(j + 1) * qb, :] * (A_DK ** -0.5)
        kk = kb[0:kv_len]
        s1 = _nt((q * m1).astype(BF16), kk)
        s2 = _nt((q * m2).astype(BF16), kk)
        qpos = j * qb + lax.broadcasted_iota(jnp.int32, (qb, kv_len), 0)
        kpos = lax.broadcasted_iota(jnp.int32, (qb, kv_len), 1)
        mask = _chunk_id(kpos) <= _chunk_id(qpos)
        s1 = jnp.where(mask, s1, -jnp.inf)
        s2 = jnp.where(mask, s2, -jnp.inf)
        e1 = jnp.exp(s1 - jnp.max(s1, axis=-1, keepdims=True))
        e2 = jnp.exp(s2 - jnp.max(s2, axis=-1, keepdims=True))
        r1 = 1.0 / jnp.sum(e1, axis=-1, keepdims=True)
        r2 = lam / jnp.sum(e2, axis=-1, keepdims=True)
        w = (e1 * r1 - e2 * r2).astype(BF16)
        o = jnp.dot(w, vb[0:kv_len], preferred_element_type=F32)
        o_ref[j * qb:(j + 1) * qb, :] = _rms(o, gain)


def _attn_prompt(z4, lam_params, a_subln, lam_init):
    _, b, t, _ = z4.shape
    qb = _blk(t, 256)

    def zspec(col):
        return pl.BlockSpec((None, None, t, HEAD_W), lambda bi, h, col=col: (col, bi, 0, h))

    return pl.pallas_call(
        functools.partial(_attn_prompt_kernel, seq=t, qb=qb, lam_init=lam_init),
        grid=(b, A_HEADS),
        in_specs=[
            pl.BlockSpec((4, A_DK), lambda bi, h: (0, 0)),
            zspec(0), zspec(1), zspec(2),
            pl.BlockSpec((1, HEAD_W), lambda bi, h: (0, 0)),
        ],
        out_specs=pl.BlockSpec((None, t, HEAD_W), lambda bi, h: (bi, 0, h)),
        out_shape=jax.ShapeDtypeStruct((b, t, A_HEADS * HEAD_W), F32),
        compiler_params=pltpu.CompilerParams(
            dimension_semantics=("parallel", "parallel"), vmem_limit_bytes=VMEM_LIMIT),
        name="attn_prompt",
    )(lam_params, z4, z4, z4, a_subln)


def _attn_sample_kernel(lam_ref, q_ref, k_ref, v_ref, pk_ref, pv_ref, sub_ref, o_ref, *, seq, past, lam_init):
    lam = _lam_from_params(lam_ref[...], lam_init)
    m1, m2 = _map_masks()
    gain = sub_ref[...] * (1.0 - lam_init)
    q = q_ref[...] * (A_DK ** -0.5)
    kn = k_ref[...].astype(BF16)
    kp = pk_ref[...].astype(BF16)
    qpos_p = past + lax.broadcasted_iota(jnp.int32, (seq, past), 0)
    kpos_p = lax.broadcasted_iota(jnp.int32, (seq, past), 1)
    mask_p = _chunk_id(kpos_p) <= _chunk_id(qpos_p)
    qpos_n = past + lax.broadcasted_iota(jnp.int32, (seq, seq), 0)
    kpos_n = past + lax.broadcasted_iota(jnp.int32, (seq, seq), 1)
    mask_n = _chunk_id(kpos_n) <= _chunk_id(qpos_n)

    def one_map(qm):
        sp = jnp.where(mask_p, _nt(qm, kp), -jnp.inf)
        sn = jnp.where(mask_n, _nt(qm, kn), -jnp.inf)
        mx = jnp.maximum(jnp.max(sp, axis=-1, keepdims=True), jnp.max(sn, axis=-1, keepdims=True))
        ep = jnp.exp(sp - mx)
        en = jnp.exp(sn - mx)
        tot = jnp.sum(ep, axis=-1, keepdims=True) + jnp.sum(en, axis=-1, keepdims=True)
        return ep, en, tot

    ep1, en1, t1 = one_map((q * m1).astype(BF16))
    ep2, en2, t2 = one_map((q * m2).astype(BF16))
    r1 = 1.0 / t1
    r2 = lam / t2
    wp = (ep1 * r1 - ep2 * r2).astype(BF16)
    wn = (en1 * r1 - en2 * r2).astype(BF16)
    o = (jnp.dot(wp, pv_ref[...].astype(BF16), preferred_element_type=F32)
         + jnp.dot(wn, v_ref[...].astype(BF16), preferred_element_type=F32))
    o_ref[...] = _rms(o, gain)


def _attn_sample(z4, past_k, past_v, lam_params, a_subln, lam_init):
    _, b, t, _ = z4.shape
    p = past_k.shape[1]

    def zspec(col):
        return pl.BlockSpec((None, None, t, HEAD_W), lambda bi, h, col=col: (col, bi, 0, h))

    pspec = pl.BlockSpec((None, p, HEAD_W), lambda bi, h: (bi, 0, h))
    return pl.pallas_call(
        functools.partial(_attn_sample_kernel, seq=t, past=p, lam_init=lam_init),
        grid=(b, A_HEADS),
        in_specs=[
            pl.BlockSpec((4, A_DK), lambda bi, h: (0, 0)),
            zspec(0), zspec(1), zspec(2), pspec, pspec,
            pl.BlockSpec((1, HEAD_W), lambda bi, h: (0, 0)),
        ],
        out_specs=pl.BlockSpec((None, t, HEAD_W), lambda bi, h: (bi, 0, h)),
        out_shape=jax.ShapeDtypeStruct((b, t, A_HEADS * HEAD_W), F32),
        compiler_params=pltpu.CompilerParams(
            dimension_semantics=("parallel", "parallel"), vmem_limit_bytes=VMEM_LIMIT),
        name="attn_sample",
    )(lam_params, z4, z4, z4, past_k, past_v, a_subln)


def _hgrn_kernel(f_ref, q_ref, i_ref, og_ref, lbl_ref, gn_ref, s0_ref, o_ref, s_ref, st_ref, *, seq, ch):
    nb = ch // R_BLOCK
    shift = int(math.log2(R_BLOCK))
    logits = lbl_ref[...]
    ex = jnp.exp(logits - jnp.max(logits, axis=0, keepdims=True))
    lb = ex[0:1] / jnp.sum(ex, axis=0, keepdims=True)
    row = lax.broadcasted_iota(jnp.int32, (ch, ch), 0)
    col = lax.broadcasted_iota(jnp.int32, (ch, ch), 1)
    same = lax.shift_right_logical(row, shift) == lax.shift_right_logical(col, shift)
    causal = jnp.logical_and(same, col <= row)
    cum_m = causal.astype(F32)
    tot_m = same.astype(F32)
    gn = gn_ref[...]
    st_ref[...] = s0_ref[...].T

    def chunk(c, carry):
        sl = pl.ds(pl.multiple_of(c * ch, ch), ch)
        g = lb + (1.0 - lb) * jax.nn.sigmoid(f_ref[sl, :])
        logf = jnp.log(g)
        kk = 1.0 - g
        qp = q_ref[sl, :]
        qq = qp * jax.nn.sigmoid(qp)
        vv = i_ref[sl, :]
        b = jnp.dot(cum_m, logf, precision=lax.Precision.HIGHEST, preferred_element_type=F32)
        bl = jnp.dot(tot_m, logf, precision=lax.Precision.HIGHEST, preferred_element_type=F32)
        q_in = (qq * jnp.exp(b)).astype(BF16)
        k_in = (kk * jnp.exp(-b)).astype(BF16)
        k_out = (kk * jnp.exp(bl - b)).astype(BF16)
        vb = vv.astype(BF16)
        a = jnp.where(causal, _nt(q_in, k_in), 0.0)
        o_intra = jnp.dot(a.astype(BF16), vb, preferred_element_type=F32)
        st = st_ref[...]
        inter = []
        for blk in range(nb):
            r0 = blk * R_BLOCK
            inter.append(_nt(q_in[r0:r0 + R_BLOCK], st.astype(BF16)))
            dl = jnp.exp(bl[r0:r0 + 1])
            upd = lax.dot_general(vb[r0:r0 + R_BLOCK], k_out[r0:r0 + R_BLOCK],
                                  (((0,), (0,)), ((), ())), preferred_element_type=F32)
            st = st * dl + upd
        st_ref[...] = st
        o = o_intra + jnp.concatenate(inter, axis=0)
        ogp = og_ref[sl, :]
        o_ref[sl, :] = _rms(o, gn) * (ogp * jax.nn.sigmoid(ogp))
        return carry

    lax.fori_loop(0, seq // ch, chunk, 0)
    s_ref[...] = st_ref[...].T


def _hgrn(z4, r_lb_logits, r_gnorm, s0):
    _, b, t, _ = z4.shape
    assert t % R_BLOCK == 0
    ch = _blk(t, 128)
    nl = r_lb_logits.shape[0]

    def zspec(col):
        return pl.BlockSpec((None, None, t, HEAD_W), lambda bi, h, col=col: (col, bi, 0, h))

    sspec = pl.BlockSpec((None, None, HEAD_W, HEAD_W), lambda bi, h: (bi, h, 0, 0))
    return pl.pallas_call(
        functools.partial(_hgrn_kernel, seq=t, ch=ch),
        grid=(b, R_HEADS),
        in_specs=[
            zspec(3), zspec(4), zspec(5), zspec(6),
            pl.BlockSpec((nl, HEAD_W), lambda bi, h: (0, h)),
            pl.BlockSpec((1, HEAD_W), lambda bi, h: (0, 0)),
            sspec,
        ],
        out_specs=[pl.BlockSpec((None, t, HEAD_W), lambda bi, h: (bi, 0, h)), sspec],
        out_shape=[jax.ShapeDtypeStruct((b, t, R_HEADS * HEAD_W), F32),
                   jax.ShapeDtypeStruct((b, R_HEADS, HEAD_W, HEAD_W), F32)],
        scratch_shapes=[pltpu.VMEM((HEAD_W, HEAD_W), F32)],
        compiler_params=pltpu.CompilerParams(
            dimension_semantics=("parallel", "parallel"), vmem_limit_bytes=VMEM_LIMIT),
        name="hgrn2",
    )(z4, z4, z4, z4, r_lb_logits, r_gnorm, s0)


def _merge_kernel(oa_ref, or_ref, ga0_ref, ga1_ref, gb0_ref, gb1_ref, x_ref,
                  wa_ref, wb_ref, wo_ref, n2_ref, wq_ref, h_ref, xn_ref, q_ref):
    pa = jnp.dot(oa_ref[...].astype(BF16), wa_ref[...], preferred_element_type=F32)
    pb = jnp.dot(or_ref[...].astype(BF16), wb_ref[...], preferred_element_type=F32)
    ga = jnp.concatenate([ga0_ref[...], ga1_ref[...]], axis=-1)
    gb = jnp.concatenate([gb0_ref[...], gb1_ref[...]], axis=-1)
    m = jax.nn.sigmoid(ga) * pa + jax.nn.sigmoid(gb) * pb
    h = x_ref[...] + jnp.dot(m.astype(BF16), wo_ref[...], preferred_element_type=F32)
    h_ref[...] = h
    xn = _rms(h, n2_ref[...])
    xn_ref[...] = xn
    q_ref[...] = jnp.dot(xn.astype(BF16), wq_ref[...], preferred_element_type=F32)


def _merge(oa, orr, z, x, w_a, w_b, w_out, norm2, p_wq):
    n, d = x.shape
    tm = _blk(n, 512)
    qw = p_wq.shape[1]

    def zspec(col):
        return pl.BlockSpec((None, tm, IN_BLOCK_W), lambda i, col=col: (col, i, 0))

    def full(a):
        return pl.BlockSpec(a.shape, lambda i: (0,) * a.ndim, pipeline_mode=pl.Buffered(1))

    return pl.pallas_call(
        _merge_kernel,
        grid=(n // tm,),
        in_specs=[
            pl.BlockSpec((tm, oa.shape[1]), lambda i: (i, 0)),
            pl.BlockSpec((tm, orr.shape[1]), lambda i: (i, 0)),
            zspec(7), zspec(8), zspec(9), zspec(10),
            pl.BlockSpec((tm, d), lambda i: (i, 0)),
            full(w_a), full(w_b), full(w_out), full(norm2), full(p_wq),
        ],
        out_specs=[pl.BlockSpec((tm, d), lambda i: (i, 0)),
                   pl.BlockSpec((tm, d), lambda i: (i, 0)),
                   pl.BlockSpec((tm, qw), lambda i: (i, 0))],
        out_shape=[jax.ShapeDtypeStruct((n, d), F32),
                   jax.ShapeDtypeStruct((n, d), F32),
                   jax.ShapeDtypeStruct((n, qw), F32)],
        compiler_params=pltpu.CompilerParams(
            dimension_semantics=("parallel",), vmem_limit_bytes=VMEM_LIMIT),
        name="merge_proj",
    )(oa, orr, z, z, z, z, x, w_a, w_b, w_out, norm2, p_wq)


def _top16(s, order, sentinel):
    vals, picks = [], []
    for _ in range(P_TOPK):
        m = jnp.max(s, axis=0, keepdims=True)
        am = jnp.min(jnp.where(s == m, order, sentinel), axis=0, keepdims=True)
        vals.append(m)
        picks.append(am)
        s = jnp.where(order == am, -jnp.inf, s)
    return vals, picks


def _pair_candidates(tt):
    half = P_TOPK // 2
    j16 = lax.broadcasted_iota(jnp.int32, (P_TOPK, tt), 0)
    j8 = lax.broadcasted_iota(jnp.int32, (half, tt), 0)
    pieces = [j16] + [i * P_TOPK + j8 for i in range(1, half)] + [(half + j8) * P_TOPK]
    return jnp.concatenate(pieces, axis=0)


def _retrieve_kernel(q_ref, keys_ref, e_ref, g_ref, *, tt):
    half = P_TOPK // 2
    iota16 = lax.broadcasted_iota(jnp.int32, (P_TOPK, tt), 0)
    key_iota = lax.broadcasted_iota(jnp.int32, (P_NKEYS, tt), 0)
    flat = _pair_candidates(tt)
    e_rows, g_rows = [], []
    for h in range(P_HEADS):
        sv, si = [], []
        for c in range(2):
            hc = h * 2 + c
            s = _nt(keys_ref[hc], q_ref[:, hc * P_NKEYS:(hc + 1) * P_NKEYS])
            vals, idxs = _top16(s, key_iota, P_NKEYS)
            sv.append(jnp.concatenate(vals, axis=0))
            si.append(jnp.concatenate(idxs, axis=0))
        comb = jnp.concatenate(
            [sv[0][0:1] + sv[1]]
            + [sv[0][i:i + 1] + sv[1][0:half] for i in range(1, half)]
            + [sv[0][half:] + sv[1][0:1]], axis=0)
        cvals, cidx = _top16(comb, flat, P_TOPK * P_TOPK)
        for k in range(P_TOPK):
            ci = cidx[k]
            i0 = lax.shift_right_logical(ci, 4)
            i1 = jnp.bitwise_and(ci, P_TOPK - 1)
            e0 = jnp.sum(jnp.where(iota16 == i0, si[0], 0), axis=0, keepdims=True)
            e1 = jnp.sum(jnp.where(iota16 == i1, si[1], 0), axis=0, keepdims=True)
            e_rows.append(e0 * P_NKEYS + e1)
        cv = jnp.concatenate(cvals, axis=0)
        ex = jnp.exp(cv - cvals[0])
        g_rows.append(ex / jnp.sum(ex, axis=0, keepdims=True))
    e_all = jnp.concatenate(e_rows, axis=0)
    g_all = jnp.concatenate(g_rows, axis=0)
    e_ref[...] = (e_all * 4).astype(F32).T.astype(jnp.int32)
    g_ref[...] = g_all.T


def _retrieve(q, keys16):
    n, qw = q.shape
    tt = _blk(n, 128)
    return pl.pallas_call(
        functools.partial(_retrieve_kernel, tt=tt),
        grid=(n // tt,),
        in_specs=[pl.BlockSpec((tt, qw), lambda i: (i, 0)),
                  pl.BlockSpec(keys16.shape, lambda i: (0, 0, 0))],
        out_specs=[pl.BlockSpec((tt, HK), lambda i: (i, 0)),
                   pl.BlockSpec((tt, HK), lambda i: (i, 0))],
        out_shape=[jax.ShapeDtypeStruct((n, HK), jnp.int32),
                   jax.ShapeDtypeStruct((n, HK), F32)],
        compiler_params=pltpu.CompilerParams(
            dimension_semantics=("parallel",), vmem_limit_bytes=VMEM_LIMIT),
        name="peer_retrieve",
    )(q, keys16)


ROW_TILE = 8
GATHER_GROUP = 16


def _pack_table(tbl):
    e = tbl.shape[0]
    u = lax.bitcast_convert_type(tbl.astype(BF16), jnp.uint16).astype(jnp.uint32).reshape(e, 4, 2, 128)
    w = u[:, :, 0, :] | (u[:, :, 1, :] << 16)
    return lax.bitcast_convert_type(w, jnp.int32).reshape(e * 4, 128)


def _gather_rows(idx_ref, tbl_ref, slab_ref, t):
    for hk in range(HK):
        e = idx_ref[t, hk]
        slab_ref[hk * 4:(hk + 1) * 4, :] = tbl_ref[pl.ds(pl.multiple_of(e, 4), 4), :]


def _split_bf16(x):
    hi = x.astype(BF16)
    lo = (x - hi.astype(F32)).astype(BF16)
    return hi, lo


def _diag_mask(rows):
    r = lax.broadcasted_iota(jnp.int32, (rows, HK * ROW_TILE), 0)
    c = lax.broadcasted_iota(jnp.int32, (rows, HK * ROW_TILE), 1)
    return jnp.bitwise_and(c, ROW_TILE - 1) == jnp.bitwise_and(r, ROW_TILE - 1)


def _pipelined_tokens(idx_ref, tbl_ref, slabs, tail, tb):
    group = min(GATHER_GROUP, tb)
    slabs[1][...] = jnp.zeros_like(slabs[1])

    def body(i, carry):
        t0 = group * i
        for j in range(group):
            _gather_rows(idx_ref, tbl_ref, slabs[j % 2], t0 + j)
            tail(slabs[(j + 1) % 2], jnp.maximum(t0 + j - 1, 0))
        return carry

    lax.fori_loop(0, tb // group, body, 0)
    tail(slabs[1], tb - 1)


def _down_kernel(idx_ref, x_ref, gate_ref, tbl_ref, sel_ref, rep_ref, chi_ref, clo_ref,
                 slab_a, slab_b, rows_ref, *, tb):
    mask = _diag_mask(ROW_TILE)

    def tail(slab_ref, t):
        hi, lo = _split_bf16(x_ref[t])
        g = _nt(jnp.concatenate([hi, lo], axis=0), pltpu.bitcast(slab_ref[...], BF16))
        gm = jnp.where(mask, g[0:ROW_TILE] + g[ROW_TILE:], 0.0)
        rows_ref[pl.ds(t, 1), :] = jnp.sum(gm, axis=0, keepdims=True)

    _pipelined_tokens(idx_ref, tbl_ref, (slab_a, slab_b), tail, tb)
    hid = jnp.dot(rows_ref[...], sel_ref[...], precision=lax.Precision.HIGHEST, preferred_element_type=F32)
    c = gate_ref[...] * (0.5 * hid * (1.0 + lax.erf(hid * (2.0 ** -0.5))))
    chi, clo = _split_bf16(c)
    chi_ref[...] = jnp.dot(chi, rep_ref[...], preferred_element_type=F32)
    clo_ref[...] = jnp.dot(clo, rep_ref[...], preferred_element_type=F32)


def _peer_down(idx, x3, gate, tbl_packed, sel, rep):
    n = idx.shape[0]
    tb = _blk(n, 128)
    assert tb % 2 == 0
    wide = HK * ROW_TILE
    return pl.pallas_call(
        functools.partial(_down_kernel, tb=tb),
        grid=(n // tb,),
        in_specs=[
            pl.BlockSpec((tb, HK), lambda i: (i, 0), memory_space=pltpu.SMEM),
            pl.BlockSpec((tb, ROW_TILE, 128), lambda i: (i, 0, 0)),
            pl.BlockSpec((tb, HK), lambda i: (i, 0)),
            pl.BlockSpec(memory_space=pltpu.VMEM),
            pl.BlockSpec(memory_space=pltpu.VMEM),
            pl.BlockSpec(memory_space=pltpu.VMEM),
        ],
        out_specs=[pl.BlockSpec((tb, wide), lambda i: (i, 0)),
                   pl.BlockSpec((tb, wide), lambda i: (i, 0))],
        out_shape=[jax.ShapeDtypeStruct((n, wide), F32), jax.ShapeDtypeStruct((n, wide), F32)],
        scratch_shapes=[pltpu.VMEM((HK * 4, 128), jnp.int32), pltpu.VMEM((HK * 4, 128), jnp.int32),
                        pltpu.VMEM((tb, wide), F32)],
        compiler_params=pltpu.CompilerParams(
            dimension_semantics=("arbitrary",), vmem_limit_bytes=VMEM_LIMIT),
        name="peer_down",
    )(idx, x3, gate, tbl_packed, sel, rep)


def _up_kernel(idx_ref, chi_ref, clo_ref, tbl_ref, out_ref, slab_a, slab_b, *, tb):
    mask = _diag_mask(2 * ROW_TILE)
    wide = HK * ROW_TILE

    def tail(slab_ref, t):
        hi = jnp.broadcast_to(chi_ref[pl.ds(t, 1), :], (ROW_TILE, wide))
        lo = jnp.broadcast_to(clo_ref[pl.ds(t, 1), :], (ROW_TILE, wide))
        lhs = jnp.where(mask, jnp.concatenate([hi, lo], axis=0), 0.0).astype(BF16)
        o = jnp.dot(lhs, pltpu.bitcast(slab_ref[...], BF16), preferred_element_type=F32)
        out_ref[t] = o[0:ROW_TILE] + o[ROW_TILE:]

    _pipelined_tokens(idx_ref, tbl_ref, (slab_a, slab_b), tail, tb)


def _peer_up(idx, chi, clo, tbl_packed):
    n = idx.shape[0]
    tb = _blk(n, 128)
    assert tb % 2 == 0
    wide = HK * ROW_TILE
    return pl.pallas_call(
        functools.partial(_up_kernel, tb=tb),
        grid=(n // tb,),
        in_specs=[
            pl.BlockSpec((tb, HK), lambda i: (i, 0), memory_space=pltpu.SMEM),
            pl.BlockSpec((tb, wide), lambda i: (i, 0)),
            pl.BlockSpec((tb, wide), lambda i: (i, 0)),
            pl.BlockSpec(memory_space=pltpu.VMEM),
        ],
        out_specs=pl.BlockSpec((tb, ROW_TILE, 128), lambda i: (i, 0, 0)),
        out_shape=jax.ShapeDtypeStruct((n, ROW_TILE, 128), F32),
        scratch_shapes=[pltpu.VMEM((HK * 4, 128), jnp.int32), pltpu.VMEM((HK * 4, 128), jnp.int32)],
        compiler_params=pltpu.CompilerParams(
            dimension_semantics=("arbitrary",), vmem_limit_bytes=VMEM_LIMIT),
        name="peer_up",
    )(idx, chi, clo, tbl_packed)


def _final_kernel(h_ref, p_ref, g_ref, y_ref):
    y_ref[...] = _rms(h_ref[...] + p_ref[...], g_ref[...])


def _final(h, peer, gain):
    n, d = h.shape
    tm = _blk(n, 512)
    return pl.pallas_call(
        _final_kernel,
        grid=(n // tm,),
        in_specs=[pl.BlockSpec((tm, d), lambda i: (i, 0)),
                  pl.BlockSpec((tm, d), lambda i: (i, 0)),
                  pl.BlockSpec((1, d), lambda i: (0, 0))],
        out_specs=pl.BlockSpec((tm, d), lambda i: (i, 0)),
        out_shape=jax.ShapeDtypeStruct((n, d), F32),
        compiler_params=pltpu.CompilerParams(
            dimension_semantics=("parallel",), vmem_limit_bytes=VMEM_LIMIT),
        name="final_norm",
    )(h, peer, gain)


def _trunk(x, past_k, past_v, past_s, prm):
    b, t, d = x.shape
    n = b * t
    lam_init = 0.8 - 0.6 * math.exp(-0.3 * 0)
    x2 = x.reshape(n, d)
    z = _inproj(x2, prm["norm1"], prm["w_in"])
    z4 = z.reshape(N_IN_BLOCKS, b, t, IN_BLOCK_W)
    if past_k is None:
        oa = _attn_prompt(z4, prm["lam"], prm["a_subln"], lam_init)
        s0 = jnp.zeros((b, R_HEADS, HEAD_W, HEAD_W), F32)
    else:
        p = past_k.shape[1]
        oa = _attn_sample(z4, past_k.reshape(b, p, A_HEADS * HEAD_W), past_v.reshape(b, p, A_HEADS * HEAD_W),
                          prm["lam"], prm["a_subln"], lam_init)
        s0 = past_s.astype(F32)
    orr, s_new = _hgrn(z4, prm["r_lb_logits"], prm["r_gnorm"], s0)
    h1, xn2, q = _merge(oa.reshape(n, -1), orr.reshape(n, -1), z, x2,
                        prm["w_a"], prm["w_b"], prm["w_out"], prm["norm2"], prm["p_wq"])
    eidx, gate = _retrieve(q, prm["p_keys"])
    chi, clo = _peer_down(eidx, xn2.reshape(n, ROW_TILE, 128), gate, prm["down_packed"], prm["sel"], prm["rep"])
    peer = _peer_up(eidx, chi, clo, prm["up_packed"]).reshape(n, d)
    y = _final(h1, peer, prm["final_norm"]).reshape(b, t, d)
    k_new = z[1].reshape(1, b, t, A_HEADS, HEAD_W)
    v_new = z[2].reshape(1, b, t, A_HEADS, HEAD_W)
    return y, k_new, v_new, s_new[None].astype(x.dtype)


def kernel(x_prompt, x_sample, cache_k, cache_v, state_hgrn, norm1, w_in, lam_params, a_subln, r_lb_logits, r_gnorm, w_a, w_b, w_out, norm2, p_wq, p_keys, p_down, p_up, final_norm):
    assert w_in.shape[0] == 1 and x_prompt.shape[-1] == 8 * 128
    sel = (lax.broadcasted_iota(jnp.int32, (HK * ROW_TILE, HK), 0) // ROW_TILE
           == lax.broadcasted_iota(jnp.int32, (HK * ROW_TILE, HK), 1)).astype(F32)
    prm = {
        "norm1": norm1[0][None].astype(F32),
        "w_in": w_in[0].astype(BF16),
        "lam": lam_params[0].astype(F32),
        "a_subln": a_subln[0][None].astype(F32),
        "r_lb_logits": r_lb_logits.astype(F32),
        "r_gnorm": r_gnorm[0][None].astype(F32),
        "w_a": w_a[0].astype(BF16),
        "w_b": w_b[0].astype(BF16),
        "w_out": w_out[0].astype(BF16),
        "norm2": norm2[0][None].astype(F32),
        "p_wq": p_wq[0].astype(BF16),
        "p_keys": p_keys[0].reshape(P_HEADS * 2, P_NKEYS, -1).astype(F32),
        "down_packed": _pack_table(p_down[0]),
        "up_packed": _pack_table(p_up[0]),
        "sel": sel,
        "rep": sel.T.astype(BF16),
        "final_norm": final_norm[None].astype(F32),
    }
    y_p, k_p, v_p, s_p = _trunk(x_prompt, None, None, None, prm)
    y_s, k_s, v_s, s_s = _trunk(x_sample, cache_k[0], cache_v[0], state_hgrn[0], prm)
    return (y_p, y_s, k_p, v_p, s_p, k_s, v_s, s_s)
```

```python
import functools
import math

import jax
import jax.numpy as jnp
from jax import lax
from jax.experimental import pallas as pl
from jax.experimental.pallas import tpu as pltpu

F32 = jnp.float32
BF16 = jnp.bfloat16
EPS = 1e-6
CHUNK = 64
A_HEADS = 4
A_DK = 64
R_HEADS = 4
R_BLOCK = 16
P_HEADS = 8
P_NKEYS = 128
P_TOPK = 16
HK = P_HEADS * P_TOPK
HEAD_W = 128
N_IN_BLOCKS = 11
IN_BLOCK_W = 512
VMEM_LIMIT = 52 * 1024 * 1024


def _blk(n, pref):
    if n <= pref:
        return n
    b = pref
    while n % b:
        b //= 2
    assert b >= 8, (n, pref)
    return b


def _nt(a, b):
    return lax.dot_general(a, b, (((1,), (1,)), ((), ())), preferred_element_type=F32)


def _rms(x, gain):
    return x * lax.rsqrt(jnp.mean(x * x, axis=-1, keepdims=True) + EPS) * gain


K_BLOCK, V_BLOCK = 1, 2


def _inproj_kernel(x_ref, g_ref, w_ref, o_ref, k_ref, v_ref, xn_ref):
    j = pl.program_id(1)

    @pl.when(j == 0)
    def _():
        xn_ref[...] = _rms(x_ref[...], g_ref[...]).astype(BF16)

    r = jnp.dot(xn_ref[...], w_ref[...], preferred_element_type=F32)
    o_ref[...] = r

    @pl.when(j == K_BLOCK)
    def _():
        k_ref[...] = r

    @pl.when(j == V_BLOCK)
    def _():
        v_ref[...] = r


def _inproj(x, gain, w_bf16):
    n, d = x.shape
    tm = _blk(n, 2048)
    kv_spec = pl.BlockSpec((tm, IN_BLOCK_W), lambda i, j: (i, 0), pipeline_mode=pl.Buffered(1))
    return pl.pallas_call(
        _inproj_kernel,
        grid=(n // tm, N_IN_BLOCKS),
        in_specs=[
            pl.BlockSpec((tm, d), lambda i, j: (i, 0)),
            pl.BlockSpec((1, d), lambda i, j: (0, 0)),
            pl.BlockSpec((d, IN_BLOCK_W), lambda i, j: (0, j)),
        ],
        out_specs=[pl.BlockSpec((None, tm, IN_BLOCK_W), lambda i, j: (j, i, 0)), kv_spec, kv_spec],
        out_shape=[jax.ShapeDtypeStruct((N_IN_BLOCKS, n, IN_BLOCK_W), F32),
                   jax.ShapeDtypeStruct((n, IN_BLOCK_W), F32),
                   jax.ShapeDtypeStruct((n, IN_BLOCK_W), F32)],
        scratch_shapes=[pltpu.VMEM((tm, d), BF16)],
        compiler_params=pltpu.CompilerParams(
            dimension_semantics=("parallel", "arbitrary"), vmem_limit_bytes=VMEM_LIMIT),
        name="inproj",
    )(x, gain, w_bf16)


def _lam_from_params(lp, lam_init):
    a = jnp.sum(lp[0:1] * lp[1:2], axis=(0, 1), keepdims=True)
    b = jnp.sum(lp[2:3] * lp[3:4], axis=(0, 1), keepdims=True)
    return jnp.exp(a) - jnp.exp(b) + lam_init


def _map_masks():
    lane = lax.broadcasted_iota(jnp.int32, (1, HEAD_W), 1)
    m1 = (lane < A_DK).astype(F32)
    return m1, 1.0 - m1


def _chunk_id(pos):
    return lax.shift_right_logical(pos, int(math.log2(CHUNK)))


def _attn_prompt_kernel(lam_ref, q_ref, k_ref, v_ref, sub_ref, o_ref, *, seq, qb, lam_init):
    lam = _lam_from_params(lam_ref[...], lam_init)
    m1, m2 = _map_masks()
    kb = k_ref[...].astype(BF16)
    vb = v_ref[...].astype(BF16)
    gain = sub_ref[...] * (1.0 - lam_init)
    for j in range(seq // qb):
        kv_len = (j + 1) * qb
        q = q_ref[j * qb:(j + 1) * qb, :] * (A_DK ** -0.5)
        kk = kb[0:kv_len]
        s1 = _nt((q * m1).astype(BF16), kk)
        s2 = _nt((q * m2).astype(BF16), kk)
        qpos = j * qb + lax.broadcasted_iota(jnp.int32, (qb, kv_len), 0)
        kpos = lax.broadcasted_iota(jnp.int32, (qb, kv_len), 1)
        mask = _chunk_id(kpos) <= _chunk_id(qpos)
        s1 = jnp.where(mask, s1, -jnp.inf)
        s2 = jnp.where(mask, s2, -jnp.inf)
        e1 = jnp.exp(s1 - jnp.max(s1, axis=-1, keepdims=True))
        e2 = jnp.exp(s2 - jnp.max(s2, axis=-1, keepdims=True))
        r1 = 1.0 / jnp.sum(e1, axis=-1, keepdims=True)
        r2 = lam / jnp.sum(e2, axis=-1, keepdims=True)
        w = (e1 * r1 - e2 * r2).astype(BF16)
        o = jnp.dot(w, vb[0:kv_len], preferred_element_type=F32)
        o_ref[j * qb:(j + 1) * qb, :] = _rms(o, gain)


def _attn_prompt(z4, lam_params, a_subln, lam_init):
    _, b, t, _ = z4.shape
    qb = _blk(t, 256)

    def zspec(col):
        return pl.BlockSpec((None, None, t, HEAD_W), lambda bi, h, col=col: (col, bi, 0, h))

    return pl.pallas_call(
        functools.partial(_attn_prompt_kernel, seq=t, qb=qb, lam_init=lam_init),
        grid=(b, A_HEADS),
        in_specs=[
            pl.BlockSpec((4, A_DK), lambda bi, h: (0, 0)),
            zspec(0), zspec(1), zspec(2),
            pl.BlockSpec((1, HEAD_W), lambda bi, h: (0, 0)),
        ],
        out_specs=pl.BlockSpec((None, t, HEAD_W), lambda bi, h: (bi, 0, h)),
        out_shape=jax.ShapeDtypeStruct((b, t, A_HEADS * HEAD_W), F32),
        compiler_params=pltpu.CompilerParams(
            dimension_semantics=("parallel", "parallel"), vmem_limit_bytes=VMEM_LIMIT),
        name="attn_prompt",
    )(lam_params, z4, z4, z4, a_subln)


def _attn_sample_kernel(lam_ref, q_ref, k_ref, v_ref, pk_ref, pv_ref, sub_ref, o_ref, *, seq, past, lam_init):
    lam = _lam_from_params(lam_ref[...], lam_init)
    m1, m2 = _map_masks()
    gain = sub_ref[...] * (1.0 - lam_init)
    q = q_ref[...] * (A_DK ** -0.5)
    kn = k_ref[...].astype(BF16)
    kp = pk_ref[...].astype(BF16)
    qpos_p = past + lax.broadcasted_iota(jnp.int32, (seq, past), 0)
    kpos_p = lax.broadcasted_iota(jnp.int32, (seq, past), 1)
    mask_p = _chunk_id(kpos_p) <= _chunk_id(qpos_p)
    qpos_n = past + lax.broadcasted_iota(jnp.int32, (seq, seq), 0)
    kpos_n = past + lax.broadcasted_iota(jnp.int32, (seq, seq), 1)
    mask_n = _chunk_id(kpos_n) <= _chunk_id(qpos_n)

    def one_map(qm):
        sp = jnp.where(mask_p, _nt(qm, kp), -jnp.inf)
        sn = jnp.where(mask_n, _nt(qm, kn), -jnp.inf)
        mx = jnp.maximum(jnp.max(sp, axis=-1, keepdims=True), jnp.max(sn, axis=-1, keepdims=True))
        ep = jnp.exp(sp - mx)
        en = jnp.exp(sn - mx)
        tot = jnp.sum(ep, axis=-1, keepdims=True) + jnp.sum(en, axis=-1, keepdims=True)
        return ep, en, tot

    ep1, en1, t1 = one_map((q * m1).astype(BF16))
    ep2, en2, t2 = one_map((q * m2).astype(BF16))
    r1 = 1.0 / t1
    r2 = lam / t2
    wp = (ep1 * r1 - ep2 * r2).astype(BF16)
    wn = (en1 * r1 - en2 * r2).astype(BF16)
    o = (jnp.dot(wp, pv_ref[...].astype(BF16), preferred_element_type=F32)
         + jnp.dot(wn, v_ref[...].astype(BF16), preferred_element_type=F32))
    o_ref[...] = _rms(o, gain)


def _attn_sample(z4, past_k, past_v, lam_params, a_subln, lam_init):
    _, b, t, _ = z4.shape
    p = past_k.shape[1]

    def zspec(col):
        return pl.BlockSpec((None, None, t, HEAD_W), lambda bi, h, col=col: (col, bi, 0, h))

    pspec = pl.BlockSpec((None, p, HEAD_W), lambda bi, h: (bi, 0, h))
    return pl.pallas_call(
        functools.partial(_attn_sample_kernel, seq=t, past=p, lam_init=lam_init),
        grid=(b, A_HEADS),
        in_specs=[
            pl.BlockSpec((4, A_DK), lambda bi, h: (0, 0)),
            zspec(0), zspec(1), zspec(2), pspec, pspec,
            pl.BlockSpec((1, HEAD_W), lambda bi, h: (0, 0)),
        ],
        out_specs=pl.BlockSpec((None, t, HEAD_W), lambda bi, h: (bi, 0, h)),
        out_shape=jax.ShapeDtypeStruct((b, t, A_HEADS * HEAD_W), F32),
        compiler_params=pltpu.CompilerParams(
            dimension_semantics=("parallel", "parallel"), vmem_limit_bytes=VMEM_LIMIT),
        name="attn_sample",
    )(lam_params, z4, z4, z4, past_k, past_v, a_subln)


R_HEADS_PER_STEP = 2


def _hgrn_kernel(f_ref, q_ref, i_ref, og_ref, lbl_ref, gn_ref, s0_ref, o_ref, s_ref, st_ref, *, seq, ch):
    nb = ch // R_BLOCK
    shift = int(math.log2(R_BLOCK))
    logits = lbl_ref[...]
    ex = jnp.exp(logits - jnp.max(logits, axis=0, keepdims=True))
    lb_all = ex[0:1] / jnp.sum(ex, axis=0, keepdims=True)
    row = lax.broadcasted_iota(jnp.int32, (ch, ch), 0)
    col = lax.broadcasted_iota(jnp.int32, (ch, ch), 1)
    same = lax.shift_right_logical(row, shift) == lax.shift_right_logical(col, shift)
    causal = jnp.logical_and(same, col <= row)
    cum_m = causal.astype(F32)
    tot_m = same.astype(F32)
    gn = gn_ref[...]
    for hh in range(R_HEADS_PER_STEP):
        st_ref[hh] = s0_ref[hh].T

    def one_head(hh, sl):
        cs = slice(hh * HEAD_W, (hh + 1) * HEAD_W)
        lb = lb_all[:, cs]
        g = lb + (1.0 - lb) * jax.nn.sigmoid(f_ref[sl, cs])
        logf = jnp.log(g)
        kk = 1.0 - g
        qp = q_ref[sl, cs]
        qq = qp * jax.nn.sigmoid(qp)
        vv = i_ref[sl, cs]
        b = jnp.dot(cum_m, logf, precision=lax.Precision.HIGHEST, preferred_element_type=F32)
        bl = jnp.dot(tot_m, logf, precision=lax.Precision.HIGHEST, preferred_element_type=F32)
        q_in = (qq * jnp.exp(b)).astype(BF16)
        k_in = (kk * jnp.exp(-b)).astype(BF16)
        k_out = (kk * jnp.exp(bl - b)).astype(BF16)
        vb = vv.astype(BF16)
        a = jnp.where(causal, _nt(q_in, k_in), 0.0)
        o_intra = jnp.dot(a.astype(BF16), vb, preferred_element_type=F32)
        upds = [lax.dot_general(vb[r0:r0 + R_BLOCK], k_out[r0:r0 + R_BLOCK],
                                (((0,), (0,)), ((), ())), preferred_element_type=F32)
                for r0 in range(0, ch, R_BLOCK)]
        sts = [st_ref[hh]]
        for blk in range(nb):
            dl = jnp.exp(bl[blk * R_BLOCK:blk * R_BLOCK + 1])
            sts.append(sts[-1] * dl + upds[blk])
        st_ref[hh] = sts[nb]
        inter = [_nt(q_in[blk * R_BLOCK:(blk + 1) * R_BLOCK], sts[blk].astype(BF16))
                 for blk in range(nb)]
        o = o_intra + jnp.concatenate(inter, axis=0)
        ogp = og_ref[sl, cs]
        o_ref[sl, cs] = _rms(o, gn) * (ogp * jax.nn.sigmoid(ogp))

    def chunk(c, carry):
        sl = pl.ds(pl.multiple_of(c * ch, ch), ch)
        for hh in range(R_HEADS_PER_STEP):
            one_head(hh, sl)
        return carry

    lax.fori_loop(0, seq // ch, chunk, 0)
    for hh in range(R_HEADS_PER_STEP):
        s_ref[hh] = st_ref[hh].T


def _hgrn(z4, r_lb_logits, r_gnorm, s0):
    _, b, t, _ = z4.shape
    assert t % R_BLOCK == 0 and R_HEADS % R_HEADS_PER_STEP == 0
    ch = _blk(t, 128)
    nl = r_lb_logits.shape[0]
    hw = R_HEADS_PER_STEP * HEAD_W

    def zspec(col):
        return pl.BlockSpec((None, None, t, hw), lambda bi, h, col=col: (col, bi, 0, h))

    sspec = pl.BlockSpec((None, R_HEADS_PER_STEP, HEAD_W, HEAD_W), lambda bi, h: (bi, h, 0, 0))
    return pl.pallas_call(
        functools.partial(_hgrn_kernel, seq=t, ch=ch),
        grid=(b, R_HEADS // R_HEADS_PER_STEP),
        in_specs=[
            zspec(3), zspec(4), zspec(5), zspec(6),
            pl.BlockSpec((nl, hw), lambda bi, h: (0, h)),
            pl.BlockSpec((1, HEAD_W), lambda bi, h: (0, 0)),
            sspec,
        ],
        out_specs=[pl.BlockSpec((None, t, hw), lambda bi, h: (bi, 0, h)), sspec],
        out_shape=[jax.ShapeDtypeStruct((b, t, R_HEADS * HEAD_W), F32),
                   jax.ShapeDtypeStruct((b, R_HEADS, HEAD_W, HEAD_W), F32)],
        scratch_shapes=[pltpu.VMEM((R_HEADS_PER_STEP, HEAD_W, HEAD_W), F32)],
        compiler_params=pltpu.CompilerParams(
            dimension_semantics=("parallel", "parallel"), vmem_limit_bytes=VMEM_LIMIT),
        name="hgrn2",
    )(z4, z4, z4, z4, r_lb_logits, r_gnorm, s0)


def _merge_kernel(oa_ref, or_ref, ga0_ref, ga1_ref, gb0_ref, gb1_ref, x_ref,
                  wa_ref, wb_ref, wo_ref, n2_ref, wq_ref, h_ref, xn_ref, q_ref):
    pa = jnp.dot(oa_ref[...].astype(BF16), wa_ref[...], preferred_element_type=F32)
    pb = jnp.dot(or_ref[...].astype(BF16), wb_ref[...], preferred_element_type=F32)
    ga = jnp.concatenate([ga0_ref[...], ga1_ref[...]], axis=-1)
    gb = jnp.concatenate([gb0_ref[...], gb1_ref[...]], axis=-1)
    m = jax.nn.sigmoid(ga) * pa + jax.nn.sigmoid(gb) * pb
    h = x_ref[...] + jnp.dot(m.astype(BF16), wo_ref[...], preferred_element_type=F32)
    h_ref[...] = h
    xn = _rms(h, n2_ref[...])
    xn_ref[...] = xn
    q_ref[...] = jnp.dot(xn.astype(BF16), wq_ref[...], preferred_element_type=F32)


def _merge(oa, orr, z, x, w_a, w_b, w_out, norm2, p_wq):
    n, d = x.shape
    tm = _blk(n, 512)
    qw = p_wq.shape[1]

    def zspec(col):
        return pl.BlockSpec((None, tm, IN_BLOCK_W), lambda i, col=col: (col, i, 0))

    def full(a):
        return pl.BlockSpec(a.shape, lambda i: (0,) * a.ndim, pipeline_mode=pl.Buffered(1))

    return pl.pallas_call(
        _merge_kernel,
        grid=(n // tm,),
        in_specs=[
            pl.BlockSpec((tm, oa.shape[1]), lambda i: (i, 0)),
            pl.BlockSpec((tm, orr.shape[1]), lambda i: (i, 0)),
            zspec(7), zspec(8), zspec(9), zspec(10),
            pl.BlockSpec((tm, d), lambda i: (i, 0)),
            full(w_a), full(w_b), full(w_out), full(norm2), full(p_wq),
        ],
        out_specs=[pl.BlockSpec((tm, d), lambda i: (i, 0)),
                   pl.BlockSpec((tm, d), lambda i: (i, 0)),
                   pl.BlockSpec((tm, qw), lambda i: (i, 0))],
        out_shape=[jax.ShapeDtypeStruct((n, d), F32),
                   jax.ShapeDtypeStruct((n, d), F32),
                   jax.ShapeDtypeStruct((n, qw), F32)],
        compiler_params=pltpu.CompilerParams(
            dimension_semantics=("parallel",), vmem_limit_bytes=VMEM_LIMIT),
        name="merge_proj",
    )(oa, orr, z, z, z, z, x, w_a, w_b, w_out, norm2, p_wq)


def _top16(s, order, sentinel):
    vals, picks = [], []
    for _ in range(P_TOPK):
        m = jnp.max(s, axis=0, keepdims=True)
        am = jnp.min(jnp.where(s == m, order, sentinel), axis=0, keepdims=True)
        vals.append(m)
        picks.append(am)
        s = jnp.where(order == am, -jnp.inf, s)
    return vals, picks


def _pair_candidates(tt):
    half = P_TOPK // 2
    j16 = lax.broadcasted_iota(jnp.int32, (P_TOPK, tt), 0)
    j8 = lax.broadcasted_iota(jnp.int32, (half, tt), 0)
    pieces = [j16] + [i * P_TOPK + j8 for i in range(1, half)] + [(half + j8) * P_TOPK]
    return jnp.concatenate(pieces, axis=0)


def _retrieve_kernel(q_ref, keys_ref, e_ref, g_ref, *, tt):
    half = P_TOPK // 2
    iota16 = lax.broadcasted_iota(jnp.int32, (P_TOPK, tt), 0)
    key_iota = lax.broadcasted_iota(jnp.int32, (P_NKEYS, tt), 0)
    flat = _pair_candidates(tt)
    e_rows, g_rows = [], []
    for h in range(P_HEADS):
        sv, si = [], []
        for c in range(2):
            hc = h * 2 + c
            s = _nt(keys_ref[hc], q_ref[:, hc * P_NKEYS:(hc + 1) * P_NKEYS])
            vals, idxs = _top16(s, key_iota, P_NKEYS)
            sv.append(jnp.concatenate(vals, axis=0))
            si.append(jnp.concatenate(idxs, axis=0))
        comb = jnp.concatenate(
            [sv[0][0:1] + sv[1]]
            + [sv[0][i:i + 1] + sv[1][0:half] for i in range(1, half)]
            + [sv[0][half:] + sv[1][0:1]], axis=0)
        cvals, cidx = _top16(comb, flat, P_TOPK * P_TOPK)
        for k in range(P_TOPK):
            ci = cidx[k]
            i0 = lax.shift_right_logical(ci, 4)
            i1 = jnp.bitwise_and(ci, P_TOPK - 1)
            e0 = jnp.sum(jnp.where(iota16 == i0, si[0], 0), axis=0, keepdims=True)
            e1 = jnp.sum(jnp.where(iota16 == i1, si[1], 0), axis=0, keepdims=True)
            e_rows.append(e0 * P_NKEYS + e1)
        cv = jnp.concatenate(cvals, axis=0)
        ex = jnp.exp(cv - cvals[0])
        g_rows.append(ex / jnp.sum(ex, axis=0, keepdims=True))
    e_all = jnp.concatenate(e_rows, axis=0)
    g_all = jnp.concatenate(g_rows, axis=0)
    e_ref[...] = (e_all * 4).astype(F32).T.astype(jnp.int32)
    g_ref[...] = g_all.T


def _retrieve(q, keys16):
    n, qw = q.shape
    tt = _blk(n, 128)
    return pl.pallas_call(
        functools.partial(_retrieve_kernel, tt=tt),
        grid=(n // tt,),
        in_specs=[pl.BlockSpec((tt, qw), lambda i: (i, 0)),
                  pl.BlockSpec(keys16.shape, lambda i: (0, 0, 0))],
        out_specs=[pl.BlockSpec((tt, HK), lambda i: (i, 0)),
                   pl.BlockSpec((tt, HK), lambda i: (i, 0))],
        out_shape=[jax.ShapeDtypeStruct((n, HK), jnp.int32),
                   jax.ShapeDtypeStruct((n, HK), F32)],
        compiler_params=pltpu.CompilerParams(
            dimension_semantics=("parallel",), vmem_limit_bytes=VMEM_LIMIT),
        name="peer_retrieve",
    )(q, keys16)


ROW_TILE = 8
GATHER_GROUP = 16
UP_VPU_PAIRS = 64


def _pack_table(tbl):
    e = tbl.shape[0]
    u = lax.bitcast_convert_type(tbl.astype(BF16), jnp.uint16).astype(jnp.uint32).reshape(e, 4, 2, 128)
    w = u[:, :, 0, :] | (u[:, :, 1, :] << 16)
    return lax.bitcast_convert_type(w, jnp.int32).reshape(e * 4, 128)


def _expert_row(tbl_ref, e):
    return tbl_ref[pl.ds(pl.multiple_of(e, 4), 4), :]


def _gather_rows(idx_ref, tbl_ref, slab_ref, t, first, count):
    for m in range(count):
        slab_ref[m * 4:(m + 1) * 4, :] = _expert_row(tbl_ref, idx_ref[t, first + m])


def _split_bf16(x):
    hi = x.astype(BF16)
    lo = (x - hi.astype(F32)).astype(BF16)
    return hi, lo


def _diag_mask(rows, pairs):
    r = lax.broadcasted_iota(jnp.int32, (rows, pairs * ROW_TILE), 0)
    c = lax.broadcasted_iota(jnp.int32, (rows, pairs * ROW_TILE), 1)
    return jnp.bitwise_and(c, ROW_TILE - 1) == jnp.bitwise_and(r, ROW_TILE - 1)


def _pipelined_tokens(slabs, gather, tail, tb):
    group = min(GATHER_GROUP, tb)
    slabs[1][...] = jnp.zeros_like(slabs[1])

    def body(i, carry):
        t0 = group * i
        for j in range(group):
            gather(slabs[j % 2], t0 + j)
            tail(slabs[(j + 1) % 2], jnp.maximum(t0 + j - 1, 0))
        return carry

    lax.fori_loop(0, tb // group, body, 0)
    tail(slabs[1], tb - 1)


def _down_kernel(idx_ref, x_ref, gate_ref, tbl_ref, sel_ref, rep_ref, chi8_ref, clo8_ref, chi_ref, clo_ref,
                 slab_a, slab_b, rows_ref, *, tb):
    mask = _diag_mask(ROW_TILE, HK)

    def gather(slab_ref, t):
        _gather_rows(idx_ref, tbl_ref, slab_ref, t, 0, HK)

    def tail(slab_ref, t):
        hi, lo = _split_bf16(x_ref[t])
        g = _nt(jnp.concatenate([hi, lo], axis=0), pltpu.bitcast(slab_ref[...], BF16))
        gm = jnp.where(mask, g[0:ROW_TILE] + g[ROW_TILE:], 0.0)
        rows_ref[pl.ds(t, 1), :] = jnp.sum(gm, axis=0, keepdims=True)

    _pipelined_tokens((slab_a, slab_b), gather, tail, tb)
    hid = jnp.dot(rows_ref[...], sel_ref[...], precision=lax.Precision.HIGHEST, preferred_element_type=F32)
    c = gate_ref[...] * (0.5 * hid * (1.0 + lax.erf(hid * (2.0 ** -0.5))))
    chi, clo = _split_bf16(c)
    chi_ref[...] = chi.astype(F32)
    clo_ref[...] = clo.astype(F32)
    chi8_ref[...] = jnp.dot(chi, rep_ref[...], preferred_element_type=F32)
    clo8_ref[...] = jnp.dot(clo, rep_ref[...], preferred_element_type=F32)


def _peer_down(idx, x3, gate, tbl_packed, sel, rep):
    n = idx.shape[0]
    tb = _blk(n, 128)
    assert tb % 2 == 0
    wide = HK * ROW_TILE
    row_spec = pl.BlockSpec((tb, HK), lambda i: (i, 0))
    wide_spec = pl.BlockSpec((tb, wide), lambda i: (i, 0))
    return pl.pallas_call(
        functools.partial(_down_kernel, tb=tb),
        grid=(n // tb,),
        in_specs=[
            pl.BlockSpec((tb, HK), lambda i: (i, 0), memory_space=pltpu.SMEM),
            pl.BlockSpec((tb, ROW_TILE, 128), lambda i: (i, 0, 0)),
            row_spec,
            pl.BlockSpec(memory_space=pltpu.VMEM),
            pl.BlockSpec(memory_space=pltpu.VMEM),
            pl.BlockSpec(memory_space=pltpu.VMEM),
        ],
        out_specs=[wide_spec, wide_spec, row_spec, row_spec],
        out_shape=[jax.ShapeDtypeStruct((n, wide), F32), jax.ShapeDtypeStruct((n, wide), F32),
                   jax.ShapeDtypeStruct((n, HK), F32), jax.ShapeDtypeStruct((n, HK), F32)],
        scratch_shapes=[pltpu.VMEM((HK * 4, 128), jnp.int32), pltpu.VMEM((HK * 4, 128), jnp.int32),
                        pltpu.VMEM((tb, wide), F32)],
        compiler_params=pltpu.CompilerParams(
            dimension_semantics=("arbitrary",), vmem_limit_bytes=VMEM_LIMIT),
        name="peer_down",
    )(idx, x3, gate, tbl_packed, sel, rep)


def _up_kernel(idx_ref, chi8_ref, clo8_ref, chi_ref, clo_ref, tbl_ref, out_ref, slab_a, slab_b, splat_ref, *, tb):
    nv = UP_VPU_PAIRS
    nm = HK - nv
    wide = nm * ROW_TILE
    mask = _diag_mask(2 * ROW_TILE, nm)
    eye = (lax.broadcasted_iota(jnp.int32, (nv, HK), 0) == lax.broadcasted_iota(jnp.int32, (nv, HK), 1))
    ones = jnp.ones((2 * HK, 128), BF16)

    def vpu_pairs(t):
        dh = jnp.where(eye, chi_ref[pl.ds(t, 1), :], 0.0)
        dl = jnp.where(eye, clo_ref[pl.ds(t, 1), :], 0.0)
        splat_ref[...] = jnp.dot(jnp.concatenate([dh, dl], axis=1).astype(BF16), ones, preferred_element_type=F32)
        accs = [jnp.zeros((4, 128), F32) for _ in range(4)]
        for hk in range(nv):
            w = _expert_row(tbl_ref, idx_ref[t, hk])
            even = lax.bitcast_convert_type(w << 16, F32)
            odd = lax.bitcast_convert_type(w & jnp.int32(-65536), F32)
            cs = jnp.broadcast_to(splat_ref[hk:hk + 1, :], (4, 128))
            k = (hk % 2) * 2
            accs[k] = accs[k] + cs * even
            accs[k + 1] = accs[k + 1] + cs * odd
        out_ref[t, pl.ds(0, 4, stride=2), :] = accs[0] + accs[2]
        out_ref[t, pl.ds(1, 4, stride=2), :] = accs[1] + accs[3]

    def gather(slab_ref, t):
        _gather_rows(idx_ref, tbl_ref, slab_ref, t, nv, nm)
        vpu_pairs(t)

    def tail(slab_ref, t):
        hi = jnp.broadcast_to(chi8_ref[pl.ds(t, 1), nv * ROW_TILE:], (ROW_TILE, wide))
        lo = jnp.broadcast_to(clo8_ref[pl.ds(t, 1), nv * ROW_TILE:], (ROW_TILE, wide))
        lhs = jnp.where(mask, jnp.concatenate([hi, lo], axis=0), 0.0).astype(BF16)
        o = jnp.dot(lhs, pltpu.bitcast(slab_ref[...], BF16), preferred_element_type=F32)
        out_ref[t] = out_ref[t] + (o[0:ROW_TILE] + o[ROW_TILE:])

    _pipelined_tokens((slab_a, slab_b), gather, tail, tb)


def _peer_up(idx, chi8, clo8, chi, clo, tbl_packed):
    n = idx.shape[0]
    tb = _blk(n, 128)
    assert tb % 2 == 0 and UP_VPU_PAIRS % ROW_TILE == 0
    nm = HK - UP_VPU_PAIRS
    row_spec = pl.BlockSpec((tb, HK), lambda i: (i, 0))
    wide_spec = pl.BlockSpec((tb, HK * ROW_TILE), lambda i: (i, 0))
    return pl.pallas_call(
        functools.partial(_up_kernel, tb=tb),
        grid=(n // tb,),
        in_specs=[
            pl.BlockSpec((tb, HK), lambda i: (i, 0), memory_space=pltpu.SMEM),
            wide_spec, wide_spec, row_spec, row_spec,
            pl.BlockSpec(memory_space=pltpu.VMEM),
        ],
        out_specs=pl.BlockSpec((tb, ROW_TILE, 128), lambda i: (i, 0, 0)),
        out_shape=jax.ShapeDtypeStruct((n, ROW_TILE, 128), F32),
        scratch_shapes=[pltpu.VMEM((nm * 4, 128), jnp.int32), pltpu.VMEM((nm * 4, 128), jnp.int32),
                        pltpu.VMEM((UP_VPU_PAIRS, 128), F32)],
        compiler_params=pltpu.CompilerParams(
            dimension_semantics=("arbitrary",), vmem_limit_bytes=VMEM_LIMIT),
        name="peer_up",
    )(idx, chi8, clo8, chi, clo, tbl_packed)


def _final_kernel(h_ref, p_ref, g_ref, y_ref):
    y_ref[...] = _rms(h_ref[...] + p_ref[...], g_ref[...])


def _final(h, peer, gain):
    n, d = h.shape
    tm = _blk(n, 512)
    return pl.pallas_call(
        _final_kernel,
        grid=(n // tm,),
        in_specs=[pl.BlockSpec((tm, d), lambda i: (i, 0)),
                  pl.BlockSpec((tm, d), lambda i: (i, 0)),
                  pl.BlockSpec((1, d), lambda i: (0, 0))],
        out_specs=pl.BlockSpec((tm, d), lambda i: (i, 0)),
        out_shape=jax.ShapeDtypeStruct((n, d), F32),
        compiler_params=pltpu.CompilerParams(
            dimension_semantics=("parallel",), vmem_limit_bytes=VMEM_LIMIT),
        name="final_norm",
    )(h, peer, gain)


def _trunk(x, past_k, past_v, past_s, prm):
    b, t, d = x.shape
    n = b * t
    lam_init = 0.8 - 0.6 * math.exp(-0.3 * 0)
    x2 = x.reshape(n, d)
    z, k_flat, v_flat = _inproj(x2, prm["norm1"], prm["w_in"])
    z4 = z.reshape(N_IN_BLOCKS, b, t, IN_BLOCK_W)
    if past_k is None:
        oa = _attn_prompt(z4, prm["lam"], prm["a_subln"], lam_init)
        s0 = jnp.zeros((b, R_HEADS, HEAD_W, HEAD_W), F32)
    else:
        p = past_k.shape[1]
        oa = _attn_sample(z4, past_k.reshape(b, p, A_HEADS * HEAD_W), past_v.reshape(b, p, A_HEADS * HEAD_W),
                          prm["lam"], prm["a_subln"], lam_init)
        s0 = past_s.astype(F32)
    orr, s_new = _hgrn(z4, prm["r_lb_logits"], prm["r_gnorm"], s0)
    h1, xn2, q = _merge(oa.reshape(n, -1), orr.reshape(n, -1), z, x2,
                        prm["w_a"], prm["w_b"], prm["w_out"], prm["norm2"], prm["p_wq"])
    eidx, gate = _retrieve(q, prm["p_keys"])
    chi8, clo8, chi, clo = _peer_down(eidx, xn2.reshape(n, ROW_TILE, 128), gate,
                                      prm["down_packed"], prm["sel"], prm["rep"])
    peer = _peer_up(eidx, chi8, clo8, chi, clo, prm["up_packed"]).reshape(n, d)
    y = _final(h1, peer, prm["final_norm"]).reshape(b, t, d)
    k_new = k_flat.reshape(1, b, t, A_HEADS, HEAD_W)
    v_new = v_flat.reshape(1, b, t, A_HEADS, HEAD_W)
    return y, k_new, v_new, s_new[None].astype(x.dtype)


def kernel(x_prompt, x_sample, cache_k, cache_v, state_hgrn, norm1, w_in, lam_params, a_subln, r_lb_logits, r_gnorm, w_a, w_b, w_out, norm2, p_wq, p_keys, p_down, p_up, final_norm):
    assert w_in.shape[0] == 1 and x_prompt.shape[-1] == 8 * 128
    sel = (lax.broadcasted_iota(jnp.int32, (HK * ROW_TILE, HK), 0) // ROW_TILE
           == lax.broadcasted_iota(jnp.int32, (HK * ROW_TILE, HK), 1)).astype(F32)
    prm = {
        "norm1": norm1[0][None].astype(F32),
        "w_in": w_in[0].astype(BF16),
        "lam": lam_params[0].astype(F32),
        "a_subln": a_subln[0][None].astype(F32),
        "r_lb_logits": r_lb_logits.astype(F32),
        "r_gnorm": r_gnorm[0][None].astype(F32),
        "w_a": w_a[0].astype(BF16),
        "w_b": w_b[0].astype(BF16),
        "w_out": w_out[0].astype(BF16),
        "norm2": norm2[0][None].astype(F32),
        "p_wq": p_wq[0].astype(BF16),
        "p_keys": p_keys[0].reshape(P_HEADS * 2, P_NKEYS, -1).astype(F32),
        "down_packed": _pack_table(p_down[0]),
        "up_packed": _pack_table(p_up[0]),
        "sel": sel,
        "rep": sel.T.astype(BF16),
        "final_norm": final_norm[None].astype(F32),
    }
    y_p, k_p, v_p, s_p = _trunk(x_prompt, None, None, None, prm)
    y_s, k_s, v_s, s_s = _trunk(x_sample, cache_k[0], cache_v[0], state_hgrn[0], prm)
    return (y_p, y_s, k_p, v_p, s_p, k_s, v_s, s_s)
```

```python
import functools
import math

import jax
import jax.numpy as jnp
from jax import lax
from jax.experimental import pallas as pl
from jax.experimental.pallas import tpu as pltpu

F32 = jnp.float32
BF16 = jnp.bfloat16
EPS = 1e-6
CHUNK = 64
A_HEADS = 4
A_DK = 64
R_HEADS = 4
R_BLOCK = 16
P_HEADS = 8
P_NKEYS = 128
P_TOPK = 16
HK = P_HEADS * P_TOPK
HEAD_W = 128
N_IN_BLOCKS = 11
IN_BLOCK_W = 512
VMEM_LIMIT = 52 * 1024 * 1024


def _blk(n, pref):
    if n <= pref:
        return n
    b = pref
    while n % b:
        b //= 2
    assert b >= 8, (n, pref)
    return b


def _nt(a, b):
    return lax.dot_general(a, b, (((1,), (1,)), ((), ())), preferred_element_type=F32)


def _rms(x, gain):
    return x * lax.rsqrt(jnp.mean(x * x, axis=-1, keepdims=True) + EPS) * gain


K_BLOCK, V_BLOCK = 1, 2


def _inproj_kernel(x_ref, g_ref, w_ref, o_ref, k_ref, v_ref, xn_ref):
    j = pl.program_id(1)

    @pl.when(j == 0)
    def _():
        xn_ref[...] = _rms(x_ref[...], g_ref[...]).astype(BF16)

    r = jnp.dot(xn_ref[...], w_ref[...], preferred_element_type=F32)
    o_ref[...] = r

    @pl.when(j == K_BLOCK)
    def _():
        k_ref[...] = r

    @pl.when(j == V_BLOCK)
    def _():
        v_ref[...] = r


def _inproj(x, gain, w_bf16):
    n, d = x.shape
    tm = _blk(n, 2048)
    kv_spec = pl.BlockSpec((tm, IN_BLOCK_W), lambda i, j: (i, 0), pipeline_mode=pl.Buffered(1))
    return pl.pallas_call(
        _inproj_kernel,
        grid=(n // tm, N_IN_BLOCKS),
        in_specs=[
            pl.BlockSpec((tm, d), lambda i, j: (i, 0)),
            pl.BlockSpec((1, d), lambda i, j: (0, 0)),
            pl.BlockSpec((d, IN_BLOCK_W), lambda i, j: (0, j)),
        ],
        out_specs=[pl.BlockSpec((None, tm, IN_BLOCK_W), lambda i, j: (j, i, 0)), kv_spec, kv_spec],
        out_shape=[jax.ShapeDtypeStruct((N_IN_BLOCKS, n, IN_BLOCK_W), F32),
                   jax.ShapeDtypeStruct((n, IN_BLOCK_W), F32),
                   jax.ShapeDtypeStruct((n, IN_BLOCK_W), F32)],
        scratch_shapes=[pltpu.VMEM((tm, d), BF16)],
        compiler_params=pltpu.CompilerParams(
            dimension_semantics=("parallel", "arbitrary"), vmem_limit_bytes=VMEM_LIMIT),
        name="inproj",
    )(x, gain, w_bf16)


def _lam_from_params(lp, lam_init):
    a = jnp.sum(lp[0:1] * lp[1:2], axis=(0, 1), keepdims=True)
    b = jnp.sum(lp[2:3] * lp[3:4], axis=(0, 1), keepdims=True)
    return jnp.exp(a) - jnp.exp(b) + lam_init


def _map_masks():
    lane = lax.broadcasted_iota(jnp.int32, (1, HEAD_W), 1)
    m1 = (lane < A_DK).astype(F32)
    return m1, 1.0 - m1


def _chunk_id(pos):
    return lax.shift_right_logical(pos, int(math.log2(CHUNK)))


def _attn_prompt_kernel(lam_ref, q_ref, k_ref, v_ref, sub_ref, o_ref, *, seq, qb, lam_init):
    lam = _lam_from_params(lam_ref[...], lam_init)
    m1, m2 = _map_masks()
    kb = k_ref[...].astype(BF16)
    vb = v_ref[...].astype(BF16)
    gain = sub_ref[...] * (1.0 - lam_init)
    for j in range(seq // qb):
        kv_len = (j + 1) * qb
        q = q_ref[j * qb:(j + 1) * qb, :] * (A_DK ** -0.5)
        kk = kb[0:kv_len]
        s1 = _nt((q * m1).astype(BF16), kk)
        s2 = _nt((q * m2).astype(BF16), kk)
        qpos = j * qb + lax.broadcasted_iota(jnp.int32, (qb, kv_len), 0)
        kpos = lax.broadcasted_iota(jnp.int32, (qb, kv_len), 1)
        mask = _chunk_id(kpos) <= _chunk_id(qpos)
        s1 = jnp.where(mask, s1, -jnp.inf)
        s2 = jnp.where(mask, s2, -jnp.inf)
        e1 = jnp.exp(s1 - jnp.max(s1, axis=-1, keepdims=True))
        e2 = jnp.exp(s2 - jnp.max(s2, axis=-1, keepdims=True))
        r1 = 1.0 / jnp.sum(e1, axis=-1, keepdims=True)
        r2 = lam / jnp.sum(e2, axis=-1, keepdims=True)
        w = (e1 * r1 - e2 * r2).astype(BF16)
        o = jnp.dot(w, vb[0:kv_len], preferred_element_type=F32)
        o_ref[j * qb:(j + 1) * qb, :] = _rms(o, gain)


def _attn_prompt(z4, lam_params, a_subln, lam_init):
    _, b, t, _ = z4.shape
    qb = _blk(t, 256)

    def zspec(col):
        return pl.BlockSpec((None, None, t, HEAD_W), lambda bi, h, col=col: (col, bi, 0, h))

    return pl.pallas_call(
        functools.partial(_attn_prompt_kernel, seq=t, qb=qb, lam_init=lam_init),
        grid=(b, A_HEADS),
        in_specs=[
            pl.BlockSpec((4, A_DK), lambda bi, h: (0, 0)),
            zspec(0), zspec(1), zspec(2),
            pl.BlockSpec((1, HEAD_W), lambda bi, h: (0, 0)),
        ],
        out_specs=pl.BlockSpec((None, t, HEAD_W), lambda bi, h: (bi, 0, h)),
        out_shape=jax.ShapeDtypeStruct((b, t, A_HEADS * HEAD_W), F32),
        compiler_params=pltpu.CompilerParams(
            dimension_semantics=("parallel", "parallel"), vmem_limit_bytes=VMEM_LIMIT),
        name="attn_prompt",
    )(lam_params, z4, z4, z4, a_subln)


def _attn_sample_kernel(lam_ref, q_ref, k_ref, v_ref, pk_ref, pv_ref, sub_ref, o_ref, *, seq, past, lam_init):
    lam = _lam_from_params(lam_ref[...], lam_init)
    m1, m2 = _map_masks()
    gain = sub_ref[...] * (1.0 - lam_init)
    q = q_ref[...] * (A_DK ** -0.5)
    kn = k_ref[...].astype(BF16)
    kp = pk_ref[...].astype(BF16)
    qpos_p = past + lax.broadcasted_iota(jnp.int32, (seq, past), 0)
    kpos_p = lax.broadcasted_iota(jnp.int32, (seq, past), 1)
    mask_p = _chunk_id(kpos_p) <= _chunk_id(qpos_p)
    qpos_n = past + lax.broadcasted_iota(jnp.int32, (seq, seq), 0)
    kpos_n = past + lax.broadcasted_iota(jnp.int32, (seq, seq), 1)
    mask_n = _chunk_id(kpos_n) <= _chunk_id(qpos_n)

    def one_map(qm):
        sp = jnp.where(mask_p, _nt(qm, kp), -jnp.inf)
        sn = jnp.where(mask_n, _nt(qm, kn), -jnp.inf)
        mx = jnp.maximum(jnp.max(sp, axis=-1, keepdims=True), jnp.max(sn, axis=-1, keepdims=True))
        ep = jnp.exp(sp - mx)
        en = jnp.exp(sn - mx)
        tot = jnp.sum(ep, axis=-1, keepdims=True) + jnp.sum(en, axis=-1, keepdims=True)
        return ep, en, tot

    ep1, en1, t1 = one_map((q * m1).astype(BF16))
    ep2, en2, t2 = one_map((q * m2).astype(BF16))
    r1 = 1.0 / t1
    r2 = lam / t2
    wp = (ep1 * r1 - ep2 * r2).astype(BF16)
    wn = (en1 * r1 - en2 * r2).astype(BF16)
    o = (jnp.dot(wp, pv_ref[...].astype(BF16), preferred_element_type=F32)
         + jnp.dot(wn, v_ref[...].astype(BF16), preferred_element_type=F32))
    o_ref[...] = _rms(o, gain)


def _attn_sample(z4, past_k, past_v, lam_params, a_subln, lam_init):
    _, b, t, _ = z4.shape
    p = past_k.shape[1]

    def zspec(col):
        return pl.BlockSpec((None, None, t, HEAD_W), lambda bi, h, col=col: (col, bi, 0, h))

    pspec = pl.BlockSpec((None, p, HEAD_W), lambda bi, h: (bi, 0, h))
    return pl.pallas_call(
        functools.partial(_attn_sample_kernel, seq=t, past=p, lam_init=lam_init),
        grid=(b, A_HEADS),
        in_specs=[
            pl.BlockSpec((4, A_DK), lambda bi, h: (0, 0)),
            zspec(0), zspec(1), zspec(2), pspec, pspec,
            pl.BlockSpec((1, HEAD_W), lambda bi, h: (0, 0)),
        ],
        out_specs=pl.BlockSpec((None, t, HEAD_W), lambda bi, h: (bi, 0, h)),
        out_shape=jax.ShapeDtypeStruct((b, t, A_HEADS * HEAD_W), F32),
        compiler_params=pltpu.CompilerParams(
            dimension_semantics=("parallel", "parallel"), vmem_limit_bytes=VMEM_LIMIT),
        name="attn_sample",
    )(lam_params, z4, z4, z4, past_k, past_v, a_subln)


R_HEADS_PER_STEP = 2


def _hgrn_kernel(f_ref, q_ref, i_ref, og_ref, lbl_ref, gn_ref, s0_ref, o_ref, s_ref, st_ref, *, seq, ch):
    nb = ch // R_BLOCK
    shift = int(math.log2(R_BLOCK))
    logits = lbl_ref[...]
    ex = jnp.exp(logits - jnp.max(logits, axis=0, keepdims=True))
    lb_all = ex[0:1] / jnp.sum(ex, axis=0, keepdims=True)
    row = lax.broadcasted_iota(jnp.int32, (ch, ch), 0)
    col = lax.broadcasted_iota(jnp.int32, (ch, ch), 1)
    same = lax.shift_right_logical(row, shift) == lax.shift_right_logical(col, shift)
    causal = jnp.logical_and(same, col <= row)
    cum_m = causal.astype(F32)
    tot_m = same.astype(F32)
    gn = gn_ref[...]
    for hh in range(R_HEADS_PER_STEP):
        st_ref[hh] = s0_ref[hh].T

    def one_head(hh, sl):
        cs = slice(hh * HEAD_W, (hh + 1) * HEAD_W)
        lb = lb_all[:, cs]
        g = lb + (1.0 - lb) * jax.nn.sigmoid(f_ref[sl, cs])
        logf = jnp.log(g)
        kk = 1.0 - g
        qp = q_ref[sl, cs]
        qq = qp * jax.nn.sigmoid(qp)
        vv = i_ref[sl, cs]
        b = jnp.dot(cum_m, logf, precision=lax.Precision.HIGHEST, preferred_element_type=F32)
        bl = jnp.dot(tot_m, logf, precision=lax.Precision.HIGHEST, preferred_element_type=F32)
        q_in = (qq * jnp.exp(b)).astype(BF16)
        k_in = (kk * jnp.exp(-b)).astype(BF16)
        k_out = (kk * jnp.exp(bl - b)).astype(BF16)
        vb = vv.astype(BF16)
        a = jnp.where(causal, _nt(q_in, k_in), 0.0)
        o_intra = jnp.dot(a.astype(BF16), vb, preferred_element_type=F32)
        upds = [lax.dot_general(vb[r0:r0 + R_BLOCK], k_out[r0:r0 + R_BLOCK],
                                (((0,), (0,)), ((), ())), preferred_element_type=F32)
                for r0 in range(0, ch, R_BLOCK)]
        sts = [st_ref[hh]]
        for blk in range(nb):
            dl = jnp.exp(bl[blk * R_BLOCK:blk * R_BLOCK + 1])
            sts.append(sts[-1] * dl + upds[blk])
        st_ref[hh] = sts[nb]
        inter = [_nt(q_in[blk * R_BLOCK:(blk + 1) * R_BLOCK], sts[blk].astype(BF16))
                 for blk in range(nb)]
        o = o_intra + jnp.concatenate(inter, axis=0)
        ogp = og_ref[sl, cs]
        o_ref[sl, cs] = _rms(o, gn) * (ogp * jax.nn.sigmoid(ogp))

    def chunk(c, carry):
        sl = pl.ds(pl.multiple_of(c * ch, ch), ch)
        for hh in range(R_HEADS_PER_STEP):
            one_head(hh, sl)
        return carry

    lax.fori_loop(0, seq // ch, chunk, 0)
    for hh in range(R_HEADS_PER_STEP):
        s_ref[hh] = st_ref[hh].T


def _hgrn(z4, r_lb_logits, r_gnorm, s0):
    _, b, t, _ = z4.shape
    assert t % R_BLOCK == 0 and R_HEADS % R_HEADS_PER_STEP == 0
    ch = _blk(t, 128)
    nl = r_lb_logits.shape[0]
    hw = R_HEADS_PER_STEP * HEAD_W

    def zspec(col):
        return pl.BlockSpec((None, None, t, hw), lambda bi, h, col=col: (col, bi, 0, h))

    sspec = pl.BlockSpec((None, R_HEADS_PER_STEP, HEAD_W, HEAD_W), lambda bi, h: (bi, h, 0, 0))
    return pl.pallas_call(
        functools.partial(_hgrn_kernel, seq=t, ch=ch),
        grid=(b, R_HEADS // R_HEADS_PER_STEP),
        in_specs=[
            zspec(3), zspec(4), zspec(5), zspec(6),
            pl.BlockSpec((nl, hw), lambda bi, h: (0, h)),
            pl.BlockSpec((1, HEAD_W), lambda bi, h: (0, 0)),
            sspec,
        ],
        out_specs=[pl.BlockSpec((None, t, hw), lambda bi, h: (bi, 0, h)), sspec],
        out_shape=[jax.ShapeDtypeStruct((b, t, R_HEADS * HEAD_W), F32),
                   jax.ShapeDtypeStruct((b, R_HEADS, HEAD_W, HEAD_W), F32)],
        scratch_shapes=[pltpu.VMEM((R_HEADS_PER_STEP, HEAD_W, HEAD_W), F32)],
        compiler_params=pltpu.CompilerParams(
            dimension_semantics=("parallel", "parallel"), vmem_limit_bytes=VMEM_LIMIT),
        name="hgrn2",
    )(z4, z4, z4, z4, r_lb_logits, r_gnorm, s0)


def _merge_kernel(oa_ref, or_ref, ga0_ref, ga1_ref, gb0_ref, gb1_ref, x_ref,
                  wa_ref, wb_ref, wo_ref, n2_ref, wq_ref, h_ref, xn_ref, q_ref):
    pa = jnp.dot(oa_ref[...].astype(BF16), wa_ref[...], preferred_element_type=F32)
    pb = jnp.dot(or_ref[...].astype(BF16), wb_ref[...], preferred_element_type=F32)
    ga = jnp.concatenate([ga0_ref[...], ga1_ref[...]], axis=-1)
    gb = jnp.concatenate([gb0_ref[...], gb1_ref[...]], axis=-1)
    m = jax.nn.sigmoid(ga) * pa + jax.nn.sigmoid(gb) * pb
    h = x_ref[...] + jnp.dot(m.astype(BF16), wo_ref[...], preferred_element_type=F32)
    h_ref[...] = h
    xn = _rms(h, n2_ref[...])
    xn_ref[...] = xn
    q_ref[...] = jnp.dot(xn.astype(BF16), wq_ref[...], preferred_element_type=F32)


def _merge(oa, orr, z, x, w_a, w_b, w_out, norm2, p_wq):
    n, d = x.shape
    tm = _blk(n, 512)
    qw = p_wq.shape[1]

    def zspec(col):
        return pl.BlockSpec((None, tm, IN_BLOCK_W), lambda i, col=col: (col, i, 0))

    def full(a):
        return pl.BlockSpec(a.shape, lambda i: (0,) * a.ndim, pipeline_mode=pl.Buffered(1))

    return pl.pallas_call(
        _merge_kernel,
        grid=(n // tm,),
        in_specs=[
            pl.BlockSpec((tm, oa.shape[1]), lambda i: (i, 0)),
            pl.BlockSpec((tm, orr.shape[1]), lambda i: (i, 0)),
            zspec(7), zspec(8), zspec(9), zspec(10),
            pl.BlockSpec((tm, d), lambda i: (i, 0)),
            full(w_a), full(w_b), full(w_out), full(norm2), full(p_wq),
        ],
        out_specs=[pl.BlockSpec((tm, d), lambda i: (i, 0)),
                   pl.BlockSpec((tm, d), lambda i: (i, 0)),
                   pl.BlockSpec((tm, qw), lambda i: (i, 0))],
        out_shape=[jax.ShapeDtypeStruct((n, d), F32),
                   jax.ShapeDtypeStruct((n, d), F32),
                   jax.ShapeDtypeStruct((n, qw), F32)],
        compiler_params=pltpu.CompilerParams(
            dimension_semantics=("parallel",), vmem_limit_bytes=VMEM_LIMIT),
        name="merge_proj",
    )(oa, orr, z, z, z, z, x, w_a, w_b, w_out, norm2, p_wq)


def _top16(s, order, sentinel):
    vals, picks = [], []
    for _ in range(P_TOPK):
        m = jnp.max(s, axis=0, keepdims=True)
        am = jnp.min(jnp.where(s == m, order, sentinel), axis=0, keepdims=True)
        vals.append(m)
        picks.append(am)
        s = jnp.where(order == am, -jnp.inf, s)
    return vals, picks


def _pair_candidates(tt):
    half = P_TOPK // 2
    j16 = lax.broadcasted_iota(jnp.int32, (P_TOPK, tt), 0)
    j8 = lax.broadcasted_iota(jnp.int32, (half, tt), 0)
    pieces = [j16] + [i * P_TOPK + j8 for i in range(1, half)] + [(half + j8) * P_TOPK]
    return jnp.concatenate(pieces, axis=0)


def _retrieve_kernel(q_ref, keys_ref, e_ref, g_ref, *, tt):
    half = P_TOPK // 2
    iota16 = lax.broadcasted_iota(jnp.int32, (P_TOPK, tt), 0)
    key_iota = lax.broadcasted_iota(jnp.int32, (P_NKEYS, tt), 0)
    flat = _pair_candidates(tt)
    e_rows, g_rows = [], []
    for h in range(P_HEADS):
        sv, si = [], []
        for c in range(2):
            hc = h * 2 + c
            s = _nt(keys_ref[hc], q_ref[:, hc * P_NKEYS:(hc + 1) * P_NKEYS])
            vals, idxs = _top16(s, key_iota, P_NKEYS)
            sv.append(jnp.concatenate(vals, axis=0))
            si.append(jnp.concatenate(idxs, axis=0))
        comb = jnp.concatenate(
            [sv[0][0:1] + sv[1]]
            + [sv[0][i:i + 1] + sv[1][0:half] for i in range(1, half)]
            + [sv[0][half:] + sv[1][0:1]], axis=0)
        cvals, cidx = _top16(comb, flat, P_TOPK * P_TOPK)
        for k in range(P_TOPK):
            ci = cidx[k]
            i0 = lax.shift_right_logical(ci, 4)
            i1 = jnp.bitwise_and(ci, P_TOPK - 1)
            e0 = jnp.sum(jnp.where(iota16 == i0, si[0], 0), axis=0, keepdims=True)
            e1 = jnp.sum(jnp.where(iota16 == i1, si[1], 0), axis=0, keepdims=True)
            e_rows.append(e0 * P_NKEYS + e1)
        cv = jnp.concatenate(cvals, axis=0)
        ex = jnp.exp(cv - cvals[0])
        g_rows.append(ex / jnp.sum(ex, axis=0, keepdims=True))
    e_all = jnp.concatenate(e_rows, axis=0)
    g_all = jnp.concatenate(g_rows, axis=0)
    e_ref[...] = (e_all * 4).astype(F32).T.astype(jnp.int32)
    g_ref[...] = g_all.T


def _retrieve(q, keys16):
    n, qw = q.shape
    tt = _blk(n, 128)
    return pl.pallas_call(
        functools.partial(_retrieve_kernel, tt=tt),
        grid=(n // tt,),
        in_specs=[pl.BlockSpec((tt, qw), lambda i: (i, 0)),
                  pl.BlockSpec(keys16.shape, lambda i: (0, 0, 0))],
        out_specs=[pl.BlockSpec((tt, HK), lambda i: (i, 0)),
                   pl.BlockSpec((tt, HK), lambda i: (i, 0))],
        out_shape=[jax.ShapeDtypeStruct((n, HK), jnp.int32),
                   jax.ShapeDtypeStruct((n, HK), F32)],
        compiler_params=pltpu.CompilerParams(
            dimension_semantics=("parallel",), vmem_limit_bytes=VMEM_LIMIT),
        name="peer_retrieve",
    )(q, keys16)


ROW_TILE = 8
GATHER_GROUP = 16
UP_VPU_PAIRS = 64


def _pack_table(tbl):
    e = tbl.shape[0]
    u = lax.bitcast_convert_type(tbl.astype(BF16), jnp.uint16).astype(jnp.uint32).reshape(e, 4, 2, 128)
    w = u[:, :, 0, :] | (u[:, :, 1, :] << 16)
    return lax.bitcast_convert_type(w, jnp.int32).reshape(e * 4, 128)


def _expert_row(tbl_ref, e):
    return tbl_ref[pl.ds(pl.multiple_of(e, 4), 4), :]


def _gather_rows(idx_ref, tbl_ref, slab_ref, t, first, count):
    for m in range(0, count, 2):
        pair = [_expert_row(tbl_ref, idx_ref[t, first + m + k]) for k in range(2)]
        slab_ref[m * 4:(m + 2) * 4, :] = jnp.concatenate(pair, axis=0)


def _split_bf16(x):
    hi = x.astype(BF16)
    lo = (x - hi.astype(F32)).astype(BF16)
    return hi, lo


def _diag_mask(rows, pairs):
    r = lax.broadcasted_iota(jnp.int32, (rows, pairs * ROW_TILE), 0)
    c = lax.broadcasted_iota(jnp.int32, (rows, pairs * ROW_TILE), 1)
    return jnp.bitwise_and(c, ROW_TILE - 1) == jnp.bitwise_and(r, ROW_TILE - 1)


def _pipelined_tokens(slabs, gather, tail, tb):
    group = min(GATHER_GROUP, tb)
    slabs[1][...] = jnp.zeros_like(slabs[1])

    def body(i, carry):
        t0 = group * i
        for j in range(group):
            gather(slabs[j % 2], t0 + j)
            tail(slabs[(j + 1) % 2], jnp.maximum(t0 + j - 1, 0))
        return carry

    lax.fori_loop(0, tb // group, body, 0)
    tail(slabs[1], tb - 1)


def _down_kernel(idx_ref, x_ref, gate_ref, tbl_ref, sel_ref, rep_ref, chi8_ref, clo8_ref, chi_ref, clo_ref,
                 slab_a, slab_b, rows_ref, *, tb):
    mask = _diag_mask(ROW_TILE, HK)

    def gather(slab_ref, t):
        _gather_rows(idx_ref, tbl_ref, slab_ref, t, 0, HK)

    def tail(slab_ref, t):
        hi, lo = _split_bf16(x_ref[t])
        g = _nt(jnp.concatenate([hi, lo], axis=0), pltpu.bitcast(slab_ref[...], BF16))
        gm = jnp.where(mask, g[0:ROW_TILE] + g[ROW_TILE:], 0.0)
        rows_ref[pl.ds(t, 1), :] = jnp.sum(gm, axis=0, keepdims=True)

    _pipelined_tokens((slab_a, slab_b), gather, tail, tb)
    hid = jnp.dot(rows_ref[...], sel_ref[...], precision=lax.Precision.HIGHEST, preferred_element_type=F32)
    c = gate_ref[...] * (0.5 * hid * (1.0 + lax.erf(hid * (2.0 ** -0.5))))
    chi, clo = _split_bf16(c)
    chi_ref[...] = chi.astype(F32)
    clo_ref[...] = clo.astype(F32)
    chi8_ref[...] = jnp.dot(chi, rep_ref[...], preferred_element_type=F32)
    clo8_ref[...] = jnp.dot(clo, rep_ref[...], preferred_element_type=F32)


def _peer_down(idx, x3, gate, tbl_packed, sel, rep):
    n = idx.shape[0]
    tb = _blk(n, 128)
    assert tb % 2 == 0
    wide = HK * ROW_TILE
    row_spec = pl.BlockSpec((tb, HK), lambda i: (i, 0))
    wide_spec = pl.BlockSpec((tb, wide), lambda i: (i, 0))
    return pl.pallas_call(
        functools.partial(_down_kernel, tb=tb),
        grid=(n // tb,),
        in_specs=[
            pl.BlockSpec((tb, HK), lambda i: (i, 0), memory_space=pltpu.SMEM),
            pl.BlockSpec((tb, ROW_TILE, 128), lambda i: (i, 0, 0)),
            row_spec,
            pl.BlockSpec(memory_space=pltpu.VMEM),
            pl.BlockSpec(memory_space=pltpu.VMEM),
            pl.BlockSpec(memory_space=pltpu.VMEM),
        ],
        out_specs=[wide_spec, wide_spec, row_spec, row_spec],
        out_shape=[jax.ShapeDtypeStruct((n, wide), F32), jax.ShapeDtypeStruct((n, wide), F32),
                   jax.ShapeDtypeStruct((n, HK), F32), jax.ShapeDtypeStruct((n, HK), F32)],
        scratch_shapes=[pltpu.VMEM((HK * 4, 128), jnp.int32), pltpu.VMEM((HK * 4, 128), jnp.int32),
                        pltpu.VMEM((tb, wide), F32)],
        compiler_params=pltpu.CompilerParams(
            dimension_semantics=("arbitrary",), vmem_limit_bytes=VMEM_LIMIT),
        name="peer_down",
    )(idx, x3, gate, tbl_packed, sel, rep)


def _up_kernel(idx_ref, chi8_ref, clo8_ref, chi_ref, clo_ref, h_ref, g_ref, tbl_ref, out_ref,
               slab_a, slab_b, splat_ref, *, tb):
    nv = UP_VPU_PAIRS
    nm = HK - nv
    wide = nm * ROW_TILE
    mask = _diag_mask(2 * ROW_TILE, nm)
    eye = (lax.broadcasted_iota(jnp.int32, (nv, HK), 0) == lax.broadcasted_iota(jnp.int32, (nv, HK), 1))
    ones = jnp.ones((2 * HK, 128), BF16)

    def vpu_pairs(t):
        dh = jnp.where(eye, chi_ref[pl.ds(t, 1), :], 0.0)
        dl = jnp.where(eye, clo_ref[pl.ds(t, 1), :], 0.0)
        splat_ref[...] = jnp.dot(jnp.concatenate([dh, dl], axis=1).astype(BF16), ones, preferred_element_type=F32)
        accs = [jnp.zeros((4, 128), F32) for _ in range(4)]
        for hk in range(nv):
            w = _expert_row(tbl_ref, idx_ref[t, hk])
            even = lax.bitcast_convert_type(w << 16, F32)
            odd = lax.bitcast_convert_type(w & jnp.int32(-65536), F32)
            cs = jnp.broadcast_to(splat_ref[hk:hk + 1, :], (4, 128))
            k = (hk % 2) * 2
            accs[k] = accs[k] + cs * even
            accs[k + 1] = accs[k + 1] + cs * odd
        out_ref[t, pl.ds(0, 4, stride=2), :] = accs[0] + accs[2]
        out_ref[t, pl.ds(1, 4, stride=2), :] = accs[1] + accs[3]

    def gather(slab_ref, t):
        _gather_rows(idx_ref, tbl_ref, slab_ref, t, nv, nm)
        vpu_pairs(t)

    def tail(slab_ref, t):
        hi = jnp.broadcast_to(chi8_ref[pl.ds(t, 1), nv * ROW_TILE:], (ROW_TILE, wide))
        lo = jnp.broadcast_to(clo8_ref[pl.ds(t, 1), nv * ROW_TILE:], (ROW_TILE, wide))
        lhs = jnp.where(mask, jnp.concatenate([hi, lo], axis=0), 0.0).astype(BF16)
        o = jnp.dot(lhs, pltpu.bitcast(slab_ref[...], BF16), preferred_element_type=F32)
        out_ref[t] = out_ref[t] + (o[0:ROW_TILE] + o[ROW_TILE:])

    _pipelined_tokens((slab_a, slab_b), gather, tail, tb)
    v = out_ref[...] + h_ref[...]
    ms = jnp.sum(jnp.sum(v * v, axis=2, keepdims=True), axis=1, keepdims=True) * (1.0 / (ROW_TILE * 128))
    out_ref[...] = v * lax.rsqrt(ms + EPS) * g_ref[...]


def _peer_up(idx, chi8, clo8, chi, clo, h3, gain_tile, tbl_packed):
    n = idx.shape[0]
    tb = _blk(n, 128)
    assert tb % 2 == 0 and UP_VPU_PAIRS % ROW_TILE == 0
    nm = HK - UP_VPU_PAIRS
    row_spec = pl.BlockSpec((tb, HK), lambda i: (i, 0))
    wide_spec = pl.BlockSpec((tb, HK * ROW_TILE), lambda i: (i, 0))
    return pl.pallas_call(
        functools.partial(_up_kernel, tb=tb),
        grid=(n // tb,),
        in_specs=[
            pl.BlockSpec((tb, HK), lambda i: (i, 0), memory_space=pltpu.SMEM),
            wide_spec, wide_spec, row_spec, row_spec,
            pl.BlockSpec((tb, ROW_TILE, 128), lambda i: (i, 0, 0)),
            pl.BlockSpec((ROW_TILE, 128), lambda i: (0, 0)),
            pl.BlockSpec(memory_space=pltpu.VMEM),
        ],
        out_specs=pl.BlockSpec((tb, ROW_TILE, 128), lambda i: (i, 0, 0)),
        out_shape=jax.ShapeDtypeStruct((n, ROW_TILE, 128), F32),
        scratch_shapes=[pltpu.VMEM((nm * 4, 128), jnp.int32), pltpu.VMEM((nm * 4, 128), jnp.int32),
                        pltpu.VMEM((UP_VPU_PAIRS, 128), F32)],
        compiler_params=pltpu.CompilerParams(
            dimension_semantics=("arbitrary",), vmem_limit_bytes=VMEM_LIMIT),
        name="peer_up",
    )(idx, chi8, clo8, chi, clo, h3, gain_tile, tbl_packed)


def _trunk(x, past_k, past_v, past_s, prm):
    b, t, d = x.shape
    n = b * t
    lam_init = 0.8 - 0.6 * math.exp(-0.3 * 0)
    x2 = x.reshape(n, d)
    z, k_flat, v_flat = _inproj(x2, prm["norm1"], prm["w_in"])
    z4 = z.reshape(N_IN_BLOCKS, b, t, IN_BLOCK_W)
    if past_k is None:
        oa = _attn_prompt(z4, prm["lam"], prm["a_subln"], lam_init)
        s0 = jnp.zeros((b, R_HEADS, HEAD_W, HEAD_W), F32)
    else:
        p = past_k.shape[1]
        oa = _attn_sample(z4, past_k.reshape(b, p, A_HEADS * HEAD_W), past_v.reshape(b, p, A_HEADS * HEAD_W),
                          prm["lam"], prm["a_subln"], lam_init)
        s0 = past_s.astype(F32)
    orr, s_new = _hgrn(z4, prm["r_lb_logits"], prm["r_gnorm"], s0)
    h1, xn2, q = _merge(oa.reshape(n, -1), orr.reshape(n, -1), z, x2,
                        prm["w_a"], prm["w_b"], prm["w_out"], prm["norm2"], prm["p_wq"])
    eidx, gate = _retrieve(q, prm["p_keys"])
    chi8, clo8, chi, clo = _peer_down(eidx, xn2.reshape(n, ROW_TILE, 128), gate,
                                      prm["down_packed"], prm["sel"], prm["rep"])
    y = _peer_up(eidx, chi8, clo8, chi, clo, h1.reshape(n, ROW_TILE, 128), prm["final_norm"],
                 prm["up_packed"]).reshape(b, t, d)
    k_new = k_flat.reshape(1, b, t, A_HEADS, HEAD_W)
    v_new = v_flat.reshape(1, b, t, A_HEADS, HEAD_W)
    return y, k_new, v_new, s_new[None].astype(x.dtype)


def kernel(x_prompt, x_sample, cache_k, cache_v, state_hgrn, norm1, w_in, lam_params, a_subln, r_lb_logits, r_gnorm, w_a, w_b, w_out, norm2, p_wq, p_keys, p_down, p_up, final_norm):
    assert w_in.shape[0] == 1 and x_prompt.shape[-1] == 8 * 128
    sel = (lax.broadcasted_iota(jnp.int32, (HK * ROW_TILE, HK), 0) // ROW_TILE
           == lax.broadcasted_iota(jnp.int32, (HK * ROW_TILE, HK), 1)).astype(F32)
    prm = {
        "norm1": norm1[0][None].astype(F32),
        "w_in": w_in[0].astype(BF16),
        "lam": lam_params[0].astype(F32),
        "a_subln": a_subln[0][None].astype(F32),
        "r_lb_logits": r_lb_logits.astype(F32),
        "r_gnorm": r_gnorm[0][None].astype(F32),
        "w_a": w_a[0].astype(BF16),
        "w_b": w_b[0].astype(BF16),
        "w_out": w_out[0].astype(BF16),
        "norm2": norm2[0][None].astype(F32),
        "p_wq": p_wq[0].astype(BF16),
        "p_keys": p_keys[0].reshape(P_HEADS * 2, P_NKEYS, -1).astype(F32),
        "down_packed": _pack_table(p_down[0]),
        "up_packed": _pack_table(p_up[0]),
        "sel": sel,
        "rep": sel.T.astype(BF16),
        "final_norm": final_norm.reshape(ROW_TILE, 128).astype(F32),
    }
    y_p, k_p, v_p, s_p = _trunk(x_prompt, None, None, None, prm)
    y_s, k_s, v_s, s_s = _trunk(x_sample, cache_k[0], cache_v[0], state_hgrn[0], prm)
    return (y_p, y_s, k_p, v_p, s_p, k_s, v_s, s_s)
```

```python
import functools
import math

import jax
import jax.numpy as jnp
from jax import lax
from jax.experimental import pallas as pl
from jax.experimental.pallas import tpu as pltpu

F32 = jnp.float32
BF16 = jnp.bfloat16
EPS = 1e-6
CHUNK = 64
A_HEADS = 4
A_DK = 64
R_HEADS = 4
R_BLOCK = 16
P_HEADS = 8
P_NKEYS = 128
P_TOPK = 16
HK = P_HEADS * P_TOPK
HEAD_W = 128
N_IN_BLOCKS = 11
IN_BLOCK_W = 512
VMEM_LIMIT = 52 * 1024 * 1024


def _blk(n, pref):
    if n <= pref:
        return n
    b = pref
    while n % b:
        b //= 2
    assert b >= 8, (n, pref)
    return b


def _nt(a, b):
    return lax.dot_general(a, b, (((1,), (1,)), ((), ())), preferred_element_type=F32)


def _rms(x, gain):
    return x * lax.rsqrt(jnp.mean(x * x, axis=-1, keepdims=True) + EPS) * gain


K_BLOCK, V_BLOCK = 1, 2


def _inproj_kernel(x_ref, g_ref, w_ref, o_ref, k_ref, v_ref, xn_ref):
    j = pl.program_id(1)

    @pl.when(j == 0)
    def _():
        xn_ref[...] = _rms(x_ref[...], g_ref[...]).astype(BF16)

    r = jnp.dot(xn_ref[...], w_ref[...], preferred_element_type=F32)
    o_ref[...] = r

    @pl.when(j == K_BLOCK)
    def _():
        k_ref[...] = r

    @pl.when(j == V_BLOCK)
    def _():
        v_ref[...] = r


def _inproj(x, gain, w_bf16):
    n, d = x.shape
    tm = _blk(n, 2048)
    kv_spec = pl.BlockSpec((tm, IN_BLOCK_W), lambda i, j: (i, 0), pipeline_mode=pl.Buffered(1))
    return pl.pallas_call(
        _inproj_kernel,
        grid=(n // tm, N_IN_BLOCKS),
        in_specs=[
            pl.BlockSpec((tm, d), lambda i, j: (i, 0)),
            pl.BlockSpec((1, d), lambda i, j: (0, 0)),
            pl.BlockSpec((d, IN_BLOCK_W), lambda i, j: (0, j)),
        ],
        out_specs=[pl.BlockSpec((None, tm, IN_BLOCK_W), lambda i, j: (j, i, 0)), kv_spec, kv_spec],
        out_shape=[jax.ShapeDtypeStruct((N_IN_BLOCKS, n, IN_BLOCK_W), F32),
                   jax.ShapeDtypeStruct((n, IN_BLOCK_W), F32),
                   jax.ShapeDtypeStruct((n, IN_BLOCK_W), F32)],
        scratch_shapes=[pltpu.VMEM((tm, d), BF16)],
        compiler_params=pltpu.CompilerParams(
            dimension_semantics=("parallel", "arbitrary"), vmem_limit_bytes=VMEM_LIMIT),
        name="inproj",
    )(x, gain, w_bf16)


def _lam_from_params(lp, lam_init):
    a = jnp.sum(lp[0:1] * lp[1:2], axis=(0, 1), keepdims=True)
    b = jnp.sum(lp[2:3] * lp[3:4], axis=(0, 1), keepdims=True)
    return jnp.exp(a) - jnp.exp(b) + lam_init


def _map_masks():
    lane = lax.broadcasted_iota(jnp.int32, (1, HEAD_W), 1)
    m1 = (lane < A_DK).astype(F32)
    return m1, 1.0 - m1


def _chunk_id(pos):
    return lax.shift_right_logical(pos, int(math.log2(CHUNK)))


def _attn_prompt_kernel(lam_ref, q_ref, k_ref, v_ref, sub_ref, o_ref, *, seq, qb, lam_init):
    lam = _lam_from_params(lam_ref[...], lam_init)
    m1, m2 = _map_masks()
    kb = k_ref[...].astype(BF16)
    vb = v_ref[...].astype(BF16)
    gain = sub_ref[...] * (1.0 - lam_init)
    for j in range(seq // qb):
        kv_len = (j + 1) * qb
        q = q_ref[j * qb:(j + 1) * qb, :] * (A_DK ** -0.5)
        kk = kb[0:kv_len]
        s1 = _nt((q * m1).astype(BF16), kk)
        s2 = _nt((q * m2).astype(BF16), kk)
        qpos = j * qb + lax.broadcasted_iota(jnp.int32, (qb, kv_len), 0)
        kpos = lax.broadcasted_iota(jnp.int32, (qb, kv_len), 1)
        mask = _chunk_id(kpos) <= _chunk_id(qpos)
        s1 = jnp.where(mask, s1, -jnp.inf)
        s2 = jnp.where(mask, s2, -jnp.inf)
        e1 = jnp.exp(s1 - jnp.max(s1, axis=-1, keepdims=True))
        e2 = jnp.exp(s2 - jnp.max(s2, axis=-1, keepdims=True))
        r1 = 1.0 / jnp.sum(e1, axis=-1, keepdims=True)
        r2 = lam / jnp.sum(e2, axis=-1, keepdims=True)
        w = (e1 * r1 - e2 * r2).astype(BF16)
        o = jnp.dot(w, vb[0:kv_len], preferred_element_type=F32)
        o_ref[j * qb:(j + 1) * qb, :] = _rms(o, gain)


def _attn_prompt(z4, lam_params, a_subln, lam_init):
    _, b, t, _ = z4.shape
    qb = _blk(t, 256)

    def zspec(col):
        return pl.BlockSpec((None, None, t, HEAD_W), lambda bi, h, col=col: (col, bi, 0, h))

    return pl.pallas_call(
        functools.partial(_attn_prompt_kernel, seq=t, qb=qb, lam_init=lam_init),
        grid=(b, A_HEADS),
        in_specs=[
            pl.BlockSpec((4, A_DK), lambda bi, h: (0, 0)),
            zspec(0), zspec(1), zspec(2),
            pl.BlockSpec((1, HEAD_W), lambda bi, h: (0, 0)),
        ],
        out_specs=pl.BlockSpec((None, t, HEAD_W), lambda bi, h: (bi, 0, h)),
        out_shape=jax.ShapeDtypeStruct((b, t, A_HEADS * HEAD_W), F32),
        compiler_params=pltpu.CompilerParams(
            dimension_semantics=("parallel", "parallel"), vmem_limit_bytes=VMEM_LIMIT),
        name="attn_prompt",
    )(lam_params, z4, z4, z4, a_subln)


def _attn_sample_kernel(lam_ref, q_ref, k_ref, v_ref, pk_ref, pv_ref, sub_ref, o_ref, *, seq, past, lam_init):
    lam = _lam_from_params(lam_ref[...], lam_init)
    m1, m2 = _map_masks()
    gain = sub_ref[...] * (1.0 - lam_init)
    q = q_ref[...] * (A_DK ** -0.5)
    kn = k_ref[...].astype(BF16)
    kp = pk_ref[...].astype(BF16)
    qpos_p = past + lax.broadcasted_iota(jnp.int32, (seq, past), 0)
    kpos_p = lax.broadcasted_iota(jnp.int32, (seq, past), 1)
    mask_p = _chunk_id(kpos_p) <= _chunk_id(qpos_p)
    qpos_n = past + lax.broadcasted_iota(jnp.int32, (seq, seq), 0)
    kpos_n = past + lax.broadcasted_iota(jnp.int32, (seq, seq), 1)
    mask_n = _chunk_id(kpos_n) <= _chunk_id(qpos_n)

    def one_map(qm):
        sp = jnp.where(mask_p, _nt(qm, kp), -jnp.inf)
        sn = jnp.where(mask_n, _nt(qm, kn), -jnp.inf)
        mx = jnp.maximum(jnp.max(sp, axis=-1, keepdims=True), jnp.max(sn, axis=-1, keepdims=True))
        ep = jnp.exp(sp - mx)
        en = jnp.exp(sn - mx)
        tot = jnp.sum(ep, axis=-1, keepdims=True) + jnp.sum(en, axis=-1, keepdims=True)
        return ep, en, tot

    ep1, en1, t1 = one_map((q * m1).astype(BF16))
    ep2, en2, t2 = one_map((q * m2).astype(BF16))
    r1 = 1.0 / t1
    r2 = lam / t2
    wp = (ep1 * r1 - ep2 * r2).astype(BF16)
    wn = (en1 * r1 - en2 * r2).astype(BF16)
    o = (jnp.dot(wp, pv_ref[...].astype(BF16), preferred_element_type=F32)
         + jnp.dot(wn, v_ref[...].astype(BF16), preferred_element_type=F32))
    o_ref[...] = _rms(o, gain)


def _attn_sample(z4, past_k, past_v, lam_params, a_subln, lam_init):
    _, b, t, _ = z4.shape
    p = past_k.shape[1]

    def zspec(col):
        return pl.BlockSpec((None, None, t, HEAD_W), lambda bi, h, col=col: (col, bi, 0, h))

    pspec = pl.BlockSpec((None, p, HEAD_W), lambda bi, h: (bi, 0, h))
    return pl.pallas_call(
        functools.partial(_attn_sample_kernel, seq=t, past=p, lam_init=lam_init),
        grid=(b, A_HEADS),
        in_specs=[
            pl.BlockSpec((4, A_DK), lambda bi, h: (0, 0)),
            zspec(0), zspec(1), zspec(2), pspec, pspec,
            pl.BlockSpec((1, HEAD_W), lambda bi, h: (0, 0)),
        ],
        out_specs=pl.BlockSpec((None, t, HEAD_W), lambda bi, h: (bi, 0, h)),
        out_shape=jax.ShapeDtypeStruct((b, t, A_HEADS * HEAD_W), F32),
        compiler_params=pltpu.CompilerParams(
            dimension_semantics=("parallel", "parallel"), vmem_limit_bytes=VMEM_LIMIT),
        name="attn_sample",
    )(lam_params, z4, z4, z4, past_k, past_v, a_subln)


R_HEADS_PER_STEP = 2


def _hgrn_kernel(f_ref, q_ref, i_ref, og_ref, lbl_ref, gn_ref, s0_ref, o_ref, s_ref, st_ref, *, seq, ch):
    nb = ch // R_BLOCK
    shift = int(math.log2(R_BLOCK))
    logits = lbl_ref[...]
    ex = jnp.exp(logits - jnp.max(logits, axis=0, keepdims=True))
    lb_all = ex[0:1] / jnp.sum(ex, axis=0, keepdims=True)
    row = lax.broadcasted_iota(jnp.int32, (ch, ch), 0)
    col = lax.broadcasted_iota(jnp.int32, (ch, ch), 1)
    same = lax.shift_right_logical(row, shift) == lax.shift_right_logical(col, shift)
    causal = jnp.logical_and(same, col <= row)
    cum_m = causal.astype(F32)
    tot_m = same.astype(F32)
    gn = gn_ref[...]
    for hh in range(R_HEADS_PER_STEP):
        st_ref[hh] = s0_ref[hh].T

    def one_head(hh, sl):
        cs = slice(hh * HEAD_W, (hh + 1) * HEAD_W)
        lb = lb_all[:, cs]
        g = lb + (1.0 - lb) * jax.nn.sigmoid(f_ref[sl, cs])
        logf = jnp.log(g)
        kk = 1.0 - g
        qp = q_ref[sl, cs]
        qq = qp * jax.nn.sigmoid(qp)
        vv = i_ref[sl, cs]
        b = jnp.dot(cum_m, logf, precision=lax.Precision.HIGHEST, preferred_element_type=F32)
        bl = jnp.dot(tot_m, logf, precision=lax.Precision.HIGHEST, preferred_element_type=F32)
        q_in = (qq * jnp.exp(b)).astype(BF16)
        k_in = (kk * jnp.exp(-b)).astype(BF16)
        k_out = (kk * jnp.exp(bl - b)).astype(BF16)
        vb = vv.astype(BF16)
        a = jnp.where(causal, _nt(q_in, k_in), 0.0)
        o_intra = jnp.dot(a.astype(BF16), vb, preferred_element_type=F32)
        upds = [lax.dot_general(vb[r0:r0 + R_BLOCK], k_out[r0:r0 + R_BLOCK],
                                (((0,), (0,)), ((), ())), preferred_element_type=F32)
                for r0 in range(0, ch, R_BLOCK)]
        sts = [st_ref[hh]]
        for blk in range(nb):
            dl = jnp.exp(bl[blk * R_BLOCK:blk * R_BLOCK + 1])
            sts.append(sts[-1] * dl + upds[blk])
        st_ref[hh] = sts[nb]
        inter = [_nt(q_in[blk * R_BLOCK:(blk + 1) * R_BLOCK], sts[blk].astype(BF16))
                 for blk in range(nb)]
        o = o_intra + jnp.concatenate(inter, axis=0)
        ogp = og_ref[sl, cs]
        o_ref[sl, cs] = _rms(o, gn) * (ogp * jax.nn.sigmoid(ogp))

    def chunk(c, carry):
        sl = pl.ds(pl.multiple_of(c * ch, ch), ch)
        for hh in range(R_HEADS_PER_STEP):
            one_head(hh, sl)
        return carry

    lax.fori_loop(0, seq // ch, chunk, 0)
    for hh in range(R_HEADS_PER_STEP):
        s_ref[hh] = st_ref[hh].T


def _hgrn(z4, r_lb_logits, r_gnorm, s0):
    _, b, t, _ = z4.shape
    assert t % R_BLOCK == 0 and R_HEADS % R_HEADS_PER_STEP == 0
    ch = _blk(t, 128)
    nl = r_lb_logits.shape[0]
    hw = R_HEADS_PER_STEP * HEAD_W

    def zspec(col):
        return pl.BlockSpec((None, None, t, hw), lambda bi, h, col=col: (col, bi, 0, h))

    sspec = pl.BlockSpec((None, R_HEADS_PER_STEP, HEAD_W, HEAD_W), lambda bi, h: (bi, h, 0, 0))
    return pl.pallas_call(
        functools.partial(_hgrn_kernel, seq=t, ch=ch),
        grid=(b, R_HEADS // R_HEADS_PER_STEP),
        in_specs=[
            zspec(3), zspec(4), zspec(5), zspec(6),
            pl.BlockSpec((nl, hw), lambda bi, h: (0, h)),
            pl.BlockSpec((1, HEAD_W), lambda bi, h: (0, 0)),
            sspec,
        ],
        out_specs=[pl.BlockSpec((None, t, hw), lambda bi, h: (bi, 0, h)), sspec],
        out_shape=[jax.ShapeDtypeStruct((b, t, R_HEADS * HEAD_W), F32),
                   jax.ShapeDtypeStruct((b, R_HEADS, HEAD_W, HEAD_W), F32)],
        scratch_shapes=[pltpu.VMEM((R_HEADS_PER_STEP, HEAD_W, HEAD_W), F32)],
        compiler_params=pltpu.CompilerParams(
            dimension_semantics=("parallel", "parallel"), vmem_limit_bytes=VMEM_LIMIT),
        name="hgrn2",
    )(z4, z4, z4, z4, r_lb_logits, r_gnorm, s0)


def _merge_kernel(oa_ref, or_ref, ga0_ref, ga1_ref, gb0_ref, gb1_ref, x_ref,
                  wa_ref, wb_ref, wo_ref, n2_ref, wq_ref, h_ref, xn_ref, q_ref):
    pa = jnp.dot(oa_ref[...].astype(BF16), wa_ref[...], preferred_element_type=F32)
    pb = jnp.dot(or_ref[...].astype(BF16), wb_ref[...], preferred_element_type=F32)
    ga = jnp.concatenate([ga0_ref[...], ga1_ref[...]], axis=-1)
    gb = jnp.concatenate([gb0_ref[...], gb1_ref[...]], axis=-1)
    m = jax.nn.sigmoid(ga) * pa + jax.nn.sigmoid(gb) * pb
    h = x_ref[...] + jnp.dot(m.astype(BF16), wo_ref[...], preferred_element_type=F32)
    h_ref[...] = h
    xn = _rms(h, n2_ref[...])
    xn_ref[...] = xn
    q_ref[...] = jnp.dot(xn.astype(BF16), wq_ref[...], preferred_element_type=F32)


def _merge(oa, orr, z, x, w_a, w_b, w_out, norm2, p_wq):
    n, d = x.shape
    tm = _blk(n, 512)
    qw = p_wq.shape[1]

    def zspec(col):
        return pl.BlockSpec((None, tm, IN_BLOCK_W), lambda i, col=col: (col, i, 0))

    def full(a):
        return pl.BlockSpec(a.shape, lambda i: (0,) * a.ndim, pipeline_mode=pl.Buffered(1))

    return pl.pallas_call(
        _merge_kernel,
        grid=(n // tm,),
        in_specs=[
            pl.BlockSpec((tm, oa.shape[1]), lambda i: (i, 0)),
            pl.BlockSpec((tm, orr.shape[1]), lambda i: (i, 0)),
            zspec(7), zspec(8), zspec(9), zspec(10),
            pl.BlockSpec((tm, d), lambda i: (i, 0)),
            full(w_a), full(w_b), full(w_out), full(norm2), full(p_wq),
        ],
        out_specs=[pl.BlockSpec((tm, d), lambda i: (i, 0)),
                   pl.BlockSpec((tm, d), lambda i: (i, 0)),
                   pl.BlockSpec((tm, qw), lambda i: (i, 0))],
        out_shape=[jax.ShapeDtypeStruct((n, d), F32),
                   jax.ShapeDtypeStruct((n, d), F32),
                   jax.ShapeDtypeStruct((n, qw), F32)],
        compiler_params=pltpu.CompilerParams(
            dimension_semantics=("parallel",), vmem_limit_bytes=VMEM_LIMIT),
        name="merge_proj",
    )(oa, orr, z, z, z, z, x, w_a, w_b, w_out, norm2, p_wq)


def _top16(s, order, sentinel):
    vals, picks = [], []
    for _ in range(P_TOPK):
        m = jnp.max(s, axis=0, keepdims=True)
        am = jnp.min(jnp.where(s == m, order, sentinel), axis=0, keepdims=True)
        vals.append(m)
        picks.append(am)
        s = jnp.where(order == am, -jnp.inf, s)
    return vals, picks


def _pair_candidates(tt):
    half = P_TOPK // 2
    j16 = lax.broadcasted_iota(jnp.int32, (P_TOPK, tt), 0)
    j8 = lax.broadcasted_iota(jnp.int32, (half, tt), 0)
    pieces = [j16] + [i * P_TOPK + j8 for i in range(1, half)] + [(half + j8) * P_TOPK]
    return jnp.concatenate(pieces, axis=0)


def _retrieve_kernel(q_ref, keys_ref, e_ref, g_ref, *, tt):
    half = P_TOPK // 2
    iota16 = lax.broadcasted_iota(jnp.int32, (P_TOPK, tt), 0)
    key_iota = lax.broadcasted_iota(jnp.int32, (P_NKEYS, tt), 0)
    flat = _pair_candidates(tt)
    e_rows, g_rows = [], []
    for h in range(P_HEADS):
        sv, si = [], []
        for c in range(2):
            hc = h * 2 + c
            s = _nt(keys_ref[hc], q_ref[:, hc * P_NKEYS:(hc + 1) * P_NKEYS])
            vals, idxs = _top16(s, key_iota, P_NKEYS)
            sv.append(jnp.concatenate(vals, axis=0))
            si.append(jnp.concatenate(idxs, axis=0))
        comb = jnp.concatenate(
            [sv[0][0:1] + sv[1]]
            + [sv[0][i:i + 1] + sv[1][0:half] for i in range(1, half)]
            + [sv[0][half:] + sv[1][0:1]], axis=0)
        cvals, cidx = _top16(comb, flat, P_TOPK * P_TOPK)
        for k in range(P_TOPK):
            ci = cidx[k]
            i0 = lax.shift_right_logical(ci, 4)
            i1 = jnp.bitwise_and(ci, P_TOPK - 1)
            e0 = jnp.sum(jnp.where(iota16 == i0, si[0], 0), axis=0, keepdims=True)
            e1 = jnp.sum(jnp.where(iota16 == i1, si[1], 0), axis=0, keepdims=True)
            e_rows.append(e0 * P_NKEYS + e1)
        cv = jnp.concatenate(cvals, axis=0)
        ex = jnp.exp(cv - cvals[0])
        g_rows.append(ex / jnp.sum(ex, axis=0, keepdims=True))
    e_all = jnp.concatenate(e_rows, axis=0)
    g_all = jnp.concatenate(g_rows, axis=0)
    e_ref[...] = e_all * 4
    g_ref[...] = g_all.T


def _retrieve(q, keys16):
    n, qw = q.shape
    tt = _blk(n, 128)
    return pl.pallas_call(
        functools.partial(_retrieve_kernel, tt=tt),
        grid=(n // tt,),
        in_specs=[pl.BlockSpec((tt, qw), lambda i: (i, 0)),
                  pl.BlockSpec(keys16.shape, lambda i: (0, 0, 0))],
        out_specs=[pl.BlockSpec((HK, tt), lambda i: (i, 0)),
                   pl.BlockSpec((tt, HK), lambda i: (i, 0))],
        out_shape=[jax.ShapeDtypeStruct((n // tt * HK, tt), jnp.int32),
                   jax.ShapeDtypeStruct((n, HK), F32)],
        compiler_params=pltpu.CompilerParams(
            dimension_semantics=("parallel",), vmem_limit_bytes=VMEM_LIMIT),
        name="peer_retrieve",
    )(q, keys16)


ROW_TILE = 8
GATHER_GROUP = 16
UP_VPU_PAIRS = 64


def _pack_table(tbl):
    e = tbl.shape[0]
    u = lax.bitcast_convert_type(tbl.astype(BF16), jnp.uint16).astype(jnp.uint32).reshape(e, 4, 2, 128)
    w = u[:, :, 0, :] | (u[:, :, 1, :] << 16)
    return lax.bitcast_convert_type(w, jnp.int32).reshape(e * 4, 128)


def _expert_row(tbl_ref, e):
    return tbl_ref[pl.ds(pl.multiple_of(e, 4), 4), :]


def _gather_rows(read_idx, tbl_ref, slab_ref, t, first, count):
    for m in range(0, count, 2):
        pair = [_expert_row(tbl_ref, read_idx(t, first + m + k)) for k in range(2)]
        slab_ref[m * 4:(m + 2) * 4, :] = jnp.concatenate(pair, axis=0)


def _with_index_block(idx_hbm, bufs, sems, tb, body):
    i = pl.program_id(0)
    n_steps = pl.num_programs(0)
    blk = HK * tb

    def copy(step, slot):
        return pltpu.make_async_copy(idx_hbm.at[pl.ds(pl.multiple_of(step * blk, blk), blk)], bufs[slot], sems.at[slot])

    @pl.when(i == 0)
    def _():
        copy(0, 0).start()

    for slot in range(2):
        @pl.when(i % 2 == slot)
        def _(slot=slot):
            copy(i, slot).wait()

            @pl.when(i + 1 < n_steps)
            def _():
                copy(i + 1, 1 - slot).start()

            buf = bufs[slot]
            body(lambda t, hk: buf.at[pl.ds(hk * tb, tb)][t])


def _split_bf16(x):
    hi = x.astype(BF16)
    lo = (x - hi.astype(F32)).astype(BF16)
    return hi, lo


def _diag_mask(rows, pairs):
    r = lax.broadcasted_iota(jnp.int32, (rows, pairs * ROW_TILE), 0)
    c = lax.broadcasted_iota(jnp.int32, (rows, pairs * ROW_TILE), 1)
    return jnp.bitwise_and(c, ROW_TILE - 1) == jnp.bitwise_and(r, ROW_TILE - 1)


def _pipelined_tokens(slabs, gather, tail, tb):
    group = min(GATHER_GROUP, tb)
    slabs[1][...] = jnp.zeros_like(slabs[1])

    def body(i, carry):
        t0 = group * i
        for j in range(group):
            gather(slabs[j % 2], t0 + j)
            tail(slabs[(j + 1) % 2], jnp.maximum(t0 + j - 1, 0))
        return carry

    lax.fori_loop(0, tb // group, body, 0)
    tail(slabs[1], tb - 1)


def _down_kernel(idx_hbm, x_ref, gate_ref, tbl_ref, sel_ref, rep_ref, chi8_ref, clo8_ref, chi_ref, clo_ref,
                 slab_a, slab_b, rows_ref, idx_a, idx_b, sems, *, tb):
    mask = _diag_mask(ROW_TILE, HK)

    def tail(slab_ref, t):
        hi, lo = _split_bf16(x_ref[t])
        g = _nt(jnp.concatenate([hi, lo], axis=0), pltpu.bitcast(slab_ref[...], BF16))
        gm = jnp.where(mask, g[0:ROW_TILE] + g[ROW_TILE:], 0.0)
        rows_ref[pl.ds(t, 1), :] = jnp.sum(gm, axis=0, keepdims=True)

    def body(read_idx):
        def gather(slab_ref, t):
            _gather_rows(read_idx, tbl_ref, slab_ref, t, 0, HK)

        _pipelined_tokens((slab_a, slab_b), gather, tail, tb)

    _with_index_block(idx_hbm, (idx_a, idx_b), sems, tb, body)
    hid = jnp.dot(rows_ref[...], sel_ref[...], precision=lax.Precision.HIGHEST, preferred_element_type=F32)
    c = gate_ref[...] * (0.5 * hid * (1.0 + lax.erf(hid * (2.0 ** -0.5))))
    chi, clo = _split_bf16(c)
    chi_ref[...] = chi.astype(F32)
    clo_ref[...] = clo.astype(F32)
    chi8_ref[...] = jnp.dot(chi, rep_ref[...], preferred_element_type=F32)
    clo8_ref[...] = jnp.dot(clo, rep_ref[...], preferred_element_type=F32)


def _peer_down(idx, x3, gate, tbl_packed, sel, rep):
    n = x3.shape[0]
    tb = _blk(n, 128)
    assert tb % 2 == 0
    wide = HK * ROW_TILE
    row_spec = pl.BlockSpec((tb, HK), lambda i: (i, 0))
    wide_spec = pl.BlockSpec((tb, wide), lambda i: (i, 0))
    return pl.pallas_call(
        functools.partial(_down_kernel, tb=tb),
        grid=(n // tb,),
        in_specs=[
            pl.BlockSpec(memory_space=pl.ANY),
            pl.BlockSpec((tb, ROW_TILE, 128), lambda i: (i, 0, 0)),
            row_spec,
            pl.BlockSpec(memory_space=pltpu.VMEM),
            pl.BlockSpec(memory_space=pltpu.VMEM),
            pl.BlockSpec(memory_space=pltpu.VMEM),
        ],
        out_specs=[wide_spec, wide_spec, row_spec, row_spec],
        out_shape=[jax.ShapeDtypeStruct((n, wide), F32), jax.ShapeDtypeStruct((n, wide), F32),
                   jax.ShapeDtypeStruct((n, HK), F32), jax.ShapeDtypeStruct((n, HK), F32)],
        scratch_shapes=[pltpu.VMEM((HK * 4, 128), jnp.int32), pltpu.VMEM((HK * 4, 128), jnp.int32),
                        pltpu.VMEM((tb, wide), F32),
                        pltpu.SMEM((HK * tb,), jnp.int32), pltpu.SMEM((HK * tb,), jnp.int32),
                        pltpu.SemaphoreType.DMA((2,))],
        compiler_params=pltpu.CompilerParams(
            dimension_semantics=("arbitrary",), vmem_limit_bytes=VMEM_LIMIT),
        name="peer_down",
    )(idx, x3, gate, tbl_packed, sel, rep)


def _up_kernel(idx_hbm, chi8_ref, clo8_ref, chi_ref, clo_ref, h_ref, g_ref, tbl_ref, out_ref,
               slab_a, slab_b, splat_ref, idx_a, idx_b, sems, *, tb):
    nv = UP_VPU_PAIRS
    nm = HK - nv
    wide = nm * ROW_TILE
    mask = _diag_mask(2 * ROW_TILE, nm)
    eye = (lax.broadcasted_iota(jnp.int32, (nv, HK), 0) == lax.broadcasted_iota(jnp.int32, (nv, HK), 1))
    ones = jnp.ones((2 * HK, 128), BF16)

    def vpu_pairs(read_idx, t):
        dh = jnp.where(eye, chi_ref[pl.ds(t, 1), :], 0.0)
        dl = jnp.where(eye, clo_ref[pl.ds(t, 1), :], 0.0)
        splat_ref[...] = jnp.dot(jnp.concatenate([dh, dl], axis=1).astype(BF16), ones, preferred_element_type=F32)
        accs = [jnp.zeros((4, 128), F32) for _ in range(4)]
        for hk in range(nv):
            w = _expert_row(tbl_ref, read_idx(t, hk))
            even = lax.bitcast_convert_type(w << 16, F32)
            odd = lax.bitcast_convert_type(w & jnp.int32(-65536), F32)
            cs = jnp.broadcast_to(splat_ref[hk:hk + 1, :], (4, 128))
            k = (hk % 2) * 2
            accs[k] = accs[k] + cs * even
            accs[k + 1] = accs[k + 1] + cs * odd
        out_ref[t, pl.ds(0, 4, stride=2), :] = accs[0] + accs[2]
        out_ref[t, pl.ds(1, 4, stride=2), :] = accs[1] + accs[3]

    def tail(slab_ref, t):
        hi = jnp.broadcast_to(chi8_ref[pl.ds(t, 1), nv * ROW_TILE:], (ROW_TILE, wide))
        lo = jnp.broadcast_to(clo8_ref[pl.ds(t, 1), nv * ROW_TILE:], (ROW_TILE, wide))
        lhs = jnp.where(mask, jnp.concatenate([hi, lo], axis=0), 0.0).astype(BF16)
        o = jnp.dot(lhs, pltpu.bitcast(slab_ref[...], BF16), preferred_element_type=F32)
        out_ref[t] = out_ref[t] + (o[0:ROW_TILE] + o[ROW_TILE:])

    def body(read_idx):
        def gather(slab_ref, t):
            _gather_rows(read_idx, tbl_ref, slab_ref, t, nv, nm)
            vpu_pairs(read_idx, t)

        _pipelined_tokens((slab_a, slab_b), gather, tail, tb)

    _with_index_block(idx_hbm, (idx_a, idx_b), sems, tb, body)
    v = out_ref[...] + h_ref[...]
    ms = jnp.sum(jnp.sum(v * v, axis=2, keepdims=True), axis=1, keepdims=True) * (1.0 / (ROW_TILE * 128))
    out_ref[...] = v * lax.rsqrt(ms + EPS) * g_ref[...]


def _peer_up(idx, chi8, clo8, chi, clo, h3, gain_tile, tbl_packed):
    n = h3.shape[0]
    tb = _blk(n, 128)
    assert tb % 2 == 0 and UP_VPU_PAIRS % ROW_TILE == 0
    nm = HK - UP_VPU_PAIRS
    row_spec = pl.BlockSpec((tb, HK), lambda i: (i, 0))
    wide_spec = pl.BlockSpec((tb, HK * ROW_TILE), lambda i: (i, 0))
    return pl.pallas_call(
        functools.partial(_up_kernel, tb=tb),
        grid=(n // tb,),
        in_specs=[
            pl.BlockSpec(memory_space=pl.ANY),
            wide_spec, wide_spec, row_spec, row_spec,
            pl.BlockSpec((tb, ROW_TILE, 128), lambda i: (i, 0, 0)),
            pl.BlockSpec((ROW_TILE, 128), lambda i: (0, 0)),
            pl.BlockSpec(memory_space=pltpu.VMEM),
        ],
        out_specs=pl.BlockSpec((tb, ROW_TILE, 128), lambda i: (i, 0, 0)),
        out_shape=jax.ShapeDtypeStruct((n, ROW_TILE, 128), F32),
        scratch_shapes=[pltpu.VMEM((nm * 4, 128), jnp.int32), pltpu.VMEM((nm * 4, 128), jnp.int32),
                        pltpu.VMEM((UP_VPU_PAIRS, 128), F32),
                        pltpu.SMEM((HK * tb,), jnp.int32), pltpu.SMEM((HK * tb,), jnp.int32),
                        pltpu.SemaphoreType.DMA((2,))],
        compiler_params=pltpu.CompilerParams(
            dimension_semantics=("arbitrary",), vmem_limit_bytes=VMEM_LIMIT),
        name="peer_up",
    )(idx, chi8, clo8, chi, clo, h3, gain_tile, tbl_packed)


def _trunk(x, past_k, past_v, past_s, prm):
    b, t, d = x.shape
    n = b * t
    lam_init = 0.8 - 0.6 * math.exp(-0.3 * 0)
    x2 = x.reshape(n, d)
    z, k_flat, v_flat = _inproj(x2, prm["norm1"], prm["w_in"])
    z4 = z.reshape(N_IN_BLOCKS, b, t, IN_BLOCK_W)
    if past_k is None:
        oa = _attn_prompt(z4, prm["lam"], prm["a_subln"], lam_init)
        s0 = jnp.zeros((b, R_HEADS, HEAD_W, HEAD_W), F32)
    else:
        p = past_k.shape[1]
        oa = _attn_sample(z4, past_k.reshape(b, p, A_HEADS * HEAD_W), past_v.reshape(b, p, A_HEADS * HEAD_W),
                          prm["lam"], prm["a_subln"], lam_init)
        s0 = past_s.astype(F32)
    orr, s_new = _hgrn(z4, prm["r_lb_logits"], prm["r_gnorm"], s0)
    h1, xn2, q = _merge(oa.reshape(n, -1), orr.reshape(n, -1), z, x2,
                        prm["w_a"], prm["w_b"], prm["w_out"], prm["norm2"], prm["p_wq"])
    eidx, gate = _retrieve(q, prm["p_keys"])
    eidx = eidx.reshape(-1)
    chi8, clo8, chi, clo = _peer_down(eidx, xn2.reshape(n, ROW_TILE, 128), gate,
                                      prm["down_packed"], prm["sel"], prm["rep"])
    y = _peer_up(eidx, chi8, clo8, chi, clo, h1.reshape(n, ROW_TILE, 128), prm["final_norm"],
                 prm["up_packed"]).reshape(b, t, d)
    k_new = k_flat.reshape(1, b, t, A_HEADS, HEAD_W)
    v_new = v_flat.reshape(1, b, t, A_HEADS, HEAD_W)
    return y, k_new, v_new, s_new[None].astype(x.dtype)


def kernel(x_prompt, x_sample, cache_k, cache_v, state_hgrn, norm1, w_in, lam_params, a_subln, r_lb_logits, r_gnorm, w_a, w_b, w_out, norm2, p_wq, p_keys, p_down, p_up, final_norm):
    assert w_in.shape[0] == 1 and x_prompt.shape[-1] == 8 * 128
    sel = (lax.broadcasted_iota(jnp.int32, (HK * ROW_TILE, HK), 0) // ROW_TILE
           == lax.broadcasted_iota(jnp.int32, (HK * ROW_TILE, HK), 1)).astype(F32)
    prm = {
        "norm1": norm1[0][None].astype(F32),
        "w_in": w_in[0].astype(BF16),
        "lam": lam_params[0].astype(F32),
        "a_subln": a_subln[0][None].astype(F32),
        "r_lb_logits": r_lb_logits.astype(F32),
        "r_gnorm": r_gnorm[0][None].astype(F32),
        "w_a": w_a[0].astype(BF16),
        "w_b": w_b[0].astype(BF16),
        "w_out": w_out[0].astype(BF16),
        "norm2": norm2[0][None].astype(F32),
        "p_wq": p_wq[0].astype(BF16),
        "p_keys": p_keys[0].reshape(P_HEADS * 2, P_NKEYS, -1).astype(F32),
        "down_packed": _pack_table(p_down[0]),
        "up_packed": _pack_table(p_up[0]),
        "sel": sel,
        "rep": sel.T.astype(BF16),
        "final_norm": final_norm.reshape(ROW_TILE, 128).astype(F32),
    }
    y_p, k_p, v_p, s_p = _trunk(x_prompt, None, None, None, prm)
    y_s, k_s, v_s, s_s = _trunk(x_sample, cache_k[0], cache_v[0], state_hgrn[0], prm)
    return (y_p, y_s, k_p, v_p, s_p, k_s, v_s, s_s)
```

```python
import functools
import math

import jax
import jax.numpy as jnp
from jax import lax
from jax.experimental import pallas as pl
from jax.experimental.pallas import tpu as pltpu

F32 = jnp.float32
BF16 = jnp.bfloat16
EPS = 1e-6
CHUNK = 64
A_HEADS = 4
A_DK = 64
R_HEADS = 4
R_BLOCK = 16
P_HEADS = 8
P_NKEYS = 128
P_TOPK = 16
HK = P_HEADS * P_TOPK
HEAD_W = 128
N_IN_BLOCKS = 11
IN_BLOCK_W = 512
VMEM_LIMIT = 52 * 1024 * 1024


def _blk(n, pref):
    if n <= pref:
        return n
    b = pref
    while n % b:
        b //= 2
    assert b >= 8, (n, pref)
    return b


def _nt(a, b):
    return lax.dot_general(a, b, (((1,), (1,)), ((), ())), preferred_element_type=F32)


def _rms(x, gain):
    return x * lax.rsqrt(jnp.mean(x * x, axis=-1, keepdims=True) + EPS) * gain


K_BLOCK, V_BLOCK = 1, 2


def _inproj_kernel(x_ref, g_ref, w_ref, o_ref, k_ref, v_ref, xn_ref):
    j = pl.program_id(1)

    @pl.when(j == 0)
    def _():
        xn_ref[...] = _rms(x_ref[...], g_ref[...]).astype(BF16)

    r = jnp.dot(xn_ref[...], w_ref[...], preferred_element_type=F32)
    o_ref[...] = r

    @pl.when(j == K_BLOCK)
    def _():
        k_ref[...] = r

    @pl.when(j == V_BLOCK)
    def _():
        v_ref[...] = r


def _inproj(x, gain, w_bf16):
    n, d = x.shape
    tm = _blk(n, 2048)
    kv_spec = pl.BlockSpec((tm, IN_BLOCK_W), lambda i, j: (i, 0), pipeline_mode=pl.Buffered(1))
    return pl.pallas_call(
        _inproj_kernel,
        grid=(n // tm, N_IN_BLOCKS),
        in_specs=[
            pl.BlockSpec((tm, d), lambda i, j: (i, 0)),
            pl.BlockSpec((1, d), lambda i, j: (0, 0)),
            pl.BlockSpec((d, IN_BLOCK_W), lambda i, j: (0, j)),
        ],
        out_specs=[pl.BlockSpec((None, tm, IN_BLOCK_W), lambda i, j: (j, i, 0)), kv_spec, kv_spec],
        out_shape=[jax.ShapeDtypeStruct((N_IN_BLOCKS, n, IN_BLOCK_W), F32),
                   jax.ShapeDtypeStruct((n, IN_BLOCK_W), F32),
                   jax.ShapeDtypeStruct((n, IN_BLOCK_W), F32)],
        scratch_shapes=[pltpu.VMEM((tm, d), BF16)],
        compiler_params=pltpu.CompilerParams(
            dimension_semantics=("parallel", "arbitrary"), vmem_limit_bytes=VMEM_LIMIT),
        name="inproj",
    )(x, gain, w_bf16)


def _lam_from_params(lp, lam_init):
    a = jnp.sum(lp[0:1] * lp[1:2], axis=(0, 1), keepdims=True)
    b = jnp.sum(lp[2:3] * lp[3:4], axis=(0, 1), keepdims=True)
    return jnp.exp(a) - jnp.exp(b) + lam_init


def _map_masks():
    lane = lax.broadcasted_iota(jnp.int32, (1, HEAD_W), 1)
    m1 = (lane < A_DK).astype(F32)
    return m1, 1.0 - m1


def _chunk_id(pos):
    return lax.shift_right_logical(pos, int(math.log2(CHUNK)))


def _attn_prompt_kernel(lam_ref, q_ref, k_ref, v_ref, sub_ref, o_ref, *, seq, qb, lam_init):
    lam = _lam_from_params(lam_ref[...], lam_init)
    m1, m2 = _map_masks()
    kb = k_ref[...].astype(BF16)
    vb = v_ref[...].astype(BF16)
    gain = sub_ref[...] * (1.0 - lam_init)
    for j in range(seq // qb):
        kv_len = (j + 1) * qb
        q = q_ref[j * qb:(j + 1) * qb, :] * (A_DK ** -0.5)
        kk = kb[0:kv_len]
        s1 = _nt((q * m1).astype(BF16), kk)
        s2 = _nt((q * m2).astype(BF16), kk)
        qpos = j * qb + lax.broadcasted_iota(jnp.int32, (qb, kv_len), 0)
        kpos = lax.broadcasted_iota(jnp.int32, (qb, kv_len), 1)
        mask = _chunk_id(kpos) <= _chunk_id(qpos)
        s1 = jnp.where(mask, s1, -jnp.inf)
        s2 = jnp.where(mask, s2, -jnp.inf)
        e1 = jnp.exp(s1 - jnp.max(s1, axis=-1, keepdims=True))
        e2 = jnp.exp(s2 - jnp.max(s2, axis=-1, keepdims=True))
        r1 = 1.0 / jnp.sum(e1, axis=-1, keepdims=True)
        r2 = lam / jnp.sum(e2, axis=-1, keepdims=True)
        w = (e1 * r1 - e2 * r2).astype(BF16)
        o = jnp.dot(w, vb[0:kv_len], preferred_element_type=F32)
        o_ref[j * qb:(j + 1) * qb, :] = _rms(o, gain)


def _attn_prompt(z4, lam_params, a_subln, lam_init):
    _, b, t, _ = z4.shape
    qb = _blk(t, 256)

    def zspec(col):
        return pl.BlockSpec((None, None, t, HEAD_W), lambda bi, h, col=col: (col, bi, 0, h))

    return pl.pallas_call(
        functools.partial(_attn_prompt_kernel, seq=t, qb=qb, lam_init=lam_init),
        grid=(b, A_HEADS),
        in_specs=[
            pl.BlockSpec((4, A_DK), lambda bi, h: (0, 0)),
            zspec(0), zspec(1), zspec(2),
            pl.BlockSpec((1, HEAD_W), lambda bi, h: (0, 0)),
        ],
        out_specs=pl.BlockSpec((None, t, HEAD_W), lambda bi, h: (bi, 0, h)),
        out_shape=jax.ShapeDtypeStruct((b, t, A_HEADS * HEAD_W), F32),
        compiler_params=pltpu.CompilerParams(
            dimension_semantics=("parallel", "parallel"), vmem_limit_bytes=VMEM_LIMIT),
        name="attn_prompt",
    )(lam_params, z4, z4, z4, a_subln)


def _attn_sample_kernel(lam_ref, q_ref, k_ref, v_ref, pk_ref, pv_ref, sub_ref, o_ref, *, seq, past, lam_init):
    lam = _lam_from_params(lam_ref[...], lam_init)
    m1, m2 = _map_masks()
    gain = sub_ref[...] * (1.0 - lam_init)
    q = q_ref[...] * (A_DK ** -0.5)
    kn = k_ref[...].astype(BF16)
    kp = pk_ref[...].astype(BF16)
    qpos_p = past + lax.broadcasted_iota(jnp.int32, (seq, past), 0)
    kpos_p = lax.broadcasted_iota(jnp.int32, (seq, past), 1)
    mask_p = _chunk_id(kpos_p) <= _chunk_id(qpos_p)
    qpos_n = past + lax.broadcasted_iota(jnp.int32, (seq, seq), 0)
    kpos_n = past + lax.broadcasted_iota(jnp.int32, (seq, seq), 1)
    mask_n = _chunk_id(kpos_n) <= _chunk_id(qpos_n)

    def one_map(qm):
        sp = jnp.where(mask_p, _nt(qm, kp), -jnp.inf)
        sn = jnp.where(mask_n, _nt(qm, kn), -jnp.inf)
        mx = jnp.maximum(jnp.max(sp, axis=-1, keepdims=True), jnp.max(sn, axis=-1, keepdims=True))
        ep = jnp.exp(sp - mx)
        en = jnp.exp(sn - mx)
        tot = jnp.sum(ep, axis=-1, keepdims=True) + jnp.sum(en, axis=-1, keepdims=True)
        return ep, en, tot

    ep1, en1, t1 = one_map((q * m1).astype(BF16))
    ep2, en2, t2 = one_map((q * m2).astype(BF16))
    r1 = 1.0 / t1
    r2 = lam / t2
    wp = (ep1 * r1 - ep2 * r2).astype(BF16)
    wn = (en1 * r1 - en2 * r2).astype(BF16)
    o = (jnp.dot(wp, pv_ref[...].astype(BF16), preferred_element_type=F32)
         + jnp.dot(wn, v_ref[...].astype(BF16), preferred_element_type=F32))
    o_ref[...] = _rms(o, gain)


def _attn_sample(z4, past_k, past_v, lam_params, a_subln, lam_init):
    _, b, t, _ = z4.shape
    p = past_k.shape[1]

    def zspec(col):
        return pl.BlockSpec((None, None, t, HEAD_W), lambda bi, h, col=col: (col, bi, 0, h))

    pspec = pl.BlockSpec((None, p, HEAD_W), lambda bi, h: (bi, 0, h))
    return pl.pallas_call(
        functools.partial(_attn_sample_kernel, seq=t, past=p, lam_init=lam_init),
        grid=(b, A_HEADS),
        in_specs=[
            pl.BlockSpec((4, A_DK), lambda bi, h: (0, 0)),
            zspec(0), zspec(1), zspec(2), pspec, pspec,
            pl.BlockSpec((1, HEAD_W), lambda bi, h: (0, 0)),
        ],
        out_specs=pl.BlockSpec((None, t, HEAD_W), lambda bi, h: (bi, 0, h)),
        out_shape=jax.ShapeDtypeStruct((b, t, A_HEADS * HEAD_W), F32),
        compiler_params=pltpu.CompilerParams(
            dimension_semantics=("parallel", "parallel"), vmem_limit_bytes=VMEM_LIMIT),
        name="attn_sample",
    )(lam_params, z4, z4, z4, past_k, past_v, a_subln)


R_HEADS_PER_STEP = 2


def _hgrn_kernel(f_ref, q_ref, i_ref, og_ref, lbl_ref, gn_ref, s0_ref, o_ref, s_ref, st_ref, *, seq, ch):
    nb = ch // R_BLOCK
    shift = int(math.log2(R_BLOCK))
    logits = lbl_ref[...]
    ex = jnp.exp(logits - jnp.max(logits, axis=0, keepdims=True))
    lb_all = ex[0:1] / jnp.sum(ex, axis=0, keepdims=True)
    row = lax.broadcasted_iota(jnp.int32, (ch, ch), 0)
    col = lax.broadcasted_iota(jnp.int32, (ch, ch), 1)
    same = lax.shift_right_logical(row, shift) == lax.shift_right_logical(col, shift)
    causal = jnp.logical_and(same, col <= row)
    cum_m = causal.astype(F32)
    tot_m = same.astype(F32)
    gn = gn_ref[...]
    for hh in range(R_HEADS_PER_STEP):
        st_ref[hh] = s0_ref[hh].T

    def one_head(hh, sl):
        cs = slice(hh * HEAD_W, (hh + 1) * HEAD_W)
        lb = lb_all[:, cs]
        g = lb + (1.0 - lb) * jax.nn.sigmoid(f_ref[sl, cs])
        logf = jnp.log(g)
        kk = 1.0 - g
        qp = q_ref[sl, cs]
        qq = qp * jax.nn.sigmoid(qp)
        vv = i_ref[sl, cs]
        b = jnp.dot(cum_m, logf, precision=lax.Precision.HIGHEST, preferred_element_type=F32)
        bl = jnp.dot(tot_m, logf, precision=lax.Precision.HIGHEST, preferred_element_type=F32)
        q_in = (qq * jnp.exp(b)).astype(BF16)
        k_in = (kk * jnp.exp(-b)).astype(BF16)
        k_out = (kk * jnp.exp(bl - b)).astype(BF16)
        vb = vv.astype(BF16)
        a = jnp.where(causal, _nt(q_in, k_in), 0.0)
        o_intra = jnp.dot(a.astype(BF16), vb, preferred_element_type=F32)
        upds = [lax.dot_general(vb[r0:r0 + R_BLOCK], k_out[r0:r0 + R_BLOCK],
                                (((0,), (0,)), ((), ())), preferred_element_type=F32)
                for r0 in range(0, ch, R_BLOCK)]
        sts = [st_ref[hh]]
        for blk in range(nb):
            dl = jnp.exp(bl[blk * R_BLOCK:blk * R_BLOCK + 1])
            sts.append(sts[-1] * dl + upds[blk])
        st_ref[hh] = sts[nb]
        inter = [_nt(q_in[blk * R_BLOCK:(blk + 1) * R_BLOCK], sts[blk].astype(BF16))
                 for blk in range(nb)]
        o = o_intra + jnp.concatenate(inter, axis=0)
        ogp = og_ref[sl, cs]
        o_ref[sl, cs] = _rms(o, gn) * (ogp * jax.nn.sigmoid(ogp))

    def chunk(c, carry):
        sl = pl.ds(pl.multiple_of(c * ch, ch), ch)
        for hh in range(R_HEADS_PER_STEP):
            one_head(hh, sl)
        return carry

    lax.fori_loop(0, seq // ch, chunk, 0)
    for hh in range(R_HEADS_PER_STEP):
        s_ref[hh] = st_ref[hh].T


def _hgrn(z4, r_lb_logits, r_gnorm, s0):
    _, b, t, _ = z4.shape
    assert t % R_BLOCK == 0 and R_HEADS % R_HEADS_PER_STEP == 0
    ch = _blk(t, 128)
    nl = r_lb_logits.shape[0]
    hw = R_HEADS_PER_STEP * HEAD_W

    def zspec(col):
        return pl.BlockSpec((None, None, t, hw), lambda bi, h, col=col: (col, bi, 0, h))

    sspec = pl.BlockSpec((None, R_HEADS_PER_STEP, HEAD_W, HEAD_W), lambda bi, h: (bi, h, 0, 0))
    return pl.pallas_call(
        functools.partial(_hgrn_kernel, seq=t, ch=ch),
        grid=(b, R_HEADS // R_HEADS_PER_STEP),
        in_specs=[
            zspec(3), zspec(4), zspec(5), zspec(6),
            pl.BlockSpec((nl, hw), lambda bi, h: (0, h)),
            pl.BlockSpec((1, HEAD_W), lambda bi, h: (0, 0)),
            sspec,
        ],
        out_specs=[pl.BlockSpec((None, t, hw), lambda bi, h: (bi, 0, h)), sspec],
        out_shape=[jax.ShapeDtypeStruct((b, t, R_HEADS * HEAD_W), F32),
                   jax.ShapeDtypeStruct((b, R_HEADS, HEAD_W, HEAD_W), F32)],
        scratch_shapes=[pltpu.VMEM((R_HEADS_PER_STEP, HEAD_W, HEAD_W), F32)],
        compiler_params=pltpu.CompilerParams(
            dimension_semantics=("parallel", "parallel"), vmem_limit_bytes=VMEM_LIMIT),
        name="hgrn2",
    )(z4, z4, z4, z4, r_lb_logits, r_gnorm, s0)


def _merge_kernel(oa_ref, or_ref, ga0_ref, ga1_ref, gb0_ref, gb1_ref, x_ref,
                  wa_ref, wb_ref, wo_ref, n2_ref, wq_ref, h_ref, xn_ref, q_ref):
    pa = jnp.dot(oa_ref[...].astype(BF16), wa_ref[...], preferred_element_type=F32)
    pb = jnp.dot(or_ref[...].astype(BF16), wb_ref[...], preferred_element_type=F32)
    ga = jnp.concatenate([ga0_ref[...], ga1_ref[...]], axis=-1)
    gb = jnp.concatenate([gb0_ref[...], gb1_ref[...]], axis=-1)
    m = jax.nn.sigmoid(ga) * pa + jax.nn.sigmoid(gb) * pb
    h = x_ref[...] + jnp.dot(m.astype(BF16), wo_ref[...], preferred_element_type=F32)
    h_ref[...] = h
    xn = _rms(h, n2_ref[...])
    xn_ref[...] = xn
    q_ref[...] = jnp.dot(xn.astype(BF16), wq_ref[...], preferred_element_type=F32)


def _merge(oa, orr, z, x, w_a, w_b, w_out, norm2, p_wq):
    n, d = x.shape
    tm = _blk(n, 512)
    qw = p_wq.shape[1]

    def zspec(col):
        return pl.BlockSpec((None, tm, IN_BLOCK_W), lambda i, col=col: (col, i, 0))

    def full(a):
        return pl.BlockSpec(a.shape, lambda i: (0,) * a.ndim, pipeline_mode=pl.Buffered(1))

    return pl.pallas_call(
        _merge_kernel,
        grid=(n // tm,),
        in_specs=[
            pl.BlockSpec((tm, oa.shape[1]), lambda i: (i, 0)),
            pl.BlockSpec((tm, orr.shape[1]), lambda i: (i, 0)),
            zspec(7), zspec(8), zspec(9), zspec(10),
            pl.BlockSpec((tm, d), lambda i: (i, 0)),
            full(w_a), full(w_b), full(w_out), full(norm2), full(p_wq),
        ],
        out_specs=[pl.BlockSpec((tm, d), lambda i: (i, 0)),
                   pl.BlockSpec((tm, d), lambda i: (i, 0)),
                   pl.BlockSpec((tm, qw), lambda i: (i, 0))],
        out_shape=[jax.ShapeDtypeStruct((n, d), F32),
                   jax.ShapeDtypeStruct((n, d), F32),
                   jax.ShapeDtypeStruct((n, qw), F32)],
        compiler_params=pltpu.CompilerParams(
            dimension_semantics=("parallel",), vmem_limit_bytes=VMEM_LIMIT),
        name="merge_proj",
    )(oa, orr, z, z, z, z, x, w_a, w_b, w_out, norm2, p_wq)


TIE_SHIFT = 1024.0


def _top16(s, order, sentinel):
    vals, picks = [], []
    for _ in range(P_TOPK):
        m = jnp.max(s, axis=0, keepdims=True)
        am = jnp.min(jnp.where(s == m, order, sentinel), axis=0, keepdims=True)
        vals.append(m)
        picks.append(am)
        s = jnp.where(order == am, -jnp.inf, s)
    return vals, picks, None


def _top16_no_ties(s, order, sentinel):
    del sentinel
    vals, keys = [], []
    for _ in range(P_TOPK):
        m = jnp.max(s, axis=0, keepdims=True)
        eq = s == m
        keys.append(jnp.sum(jnp.where(eq, order + TIE_SHIFT, 0.0), axis=0, keepdims=True))
        vals.append(m)
        s = jnp.where(eq, -jnp.inf, s)
    worst = functools.reduce(jnp.maximum, keys)
    return vals, [k - TIE_SHIFT for k in keys], worst >= 2.0 * TIE_SHIFT


def _pair_candidates(tt):
    half = P_TOPK // 2
    j16 = lax.broadcasted_iota(jnp.int32, (P_TOPK, tt), 0)
    j8 = lax.broadcasted_iota(jnp.int32, (half, tt), 0)
    pieces = [j16] + [i * P_TOPK + j8 for i in range(1, half)] + [(half + j8) * P_TOPK]
    return jnp.concatenate(pieces, axis=0).astype(F32)


def _retrieve_block(q_ref, keys_ref, tt, top16):
    half = P_TOPK // 2
    iota16 = lax.broadcasted_iota(jnp.int32, (P_TOPK, tt), 0).astype(F32)
    key_iota = lax.broadcasted_iota(jnp.int32, (P_NKEYS, tt), 0).astype(F32)
    flat = _pair_candidates(tt)
    e_rows, g_rows, flags = [], [], []
    for h in range(P_HEADS):
        sv, si = [], []
        for c in range(2):
            hc = h * 2 + c
            s = _nt(keys_ref[hc], q_ref[:, hc * P_NKEYS:(hc + 1) * P_NKEYS])
            vals, idxs, flag = top16(s, key_iota, float(P_NKEYS))
            flags.append(flag)
            sv.append(jnp.concatenate(vals, axis=0))
            si.append(jnp.concatenate(idxs, axis=0))
        comb = jnp.concatenate(
            [sv[0][0:1] + sv[1]]
            + [sv[0][i:i + 1] + sv[1][0:half] for i in range(1, half)]
            + [sv[0][half:] + sv[1][0:1]], axis=0)
        cvals, cidx, flag = top16(comb, flat, float(P_TOPK * P_TOPK))
        flags.append(flag)
        for k in range(P_TOPK):
            i0 = jnp.floor(cidx[k] * (1.0 / P_TOPK))
            i1 = cidx[k] - i0 * P_TOPK
            e0 = jnp.sum(jnp.where(iota16 == i0, si[0], 0.0), axis=0, keepdims=True)
            e1 = jnp.sum(jnp.where(iota16 == i1, si[1], 0.0), axis=0, keepdims=True)
            e_rows.append(e0 * P_NKEYS + e1)
        cv = jnp.concatenate(cvals, axis=0)
        ex = jnp.exp(cv - cvals[0])
        g_rows.append(ex / jnp.sum(ex, axis=0, keepdims=True))
    e_all = (jnp.concatenate(e_rows, axis=0) * 4.0).astype(jnp.int32)
    g_all = jnp.concatenate(g_rows, axis=0)
    flags = [f for f in flags if f is not None]
    return e_all, g_all, (functools.reduce(jnp.logical_or, flags) if flags else None)


def _retrieve_kernel(q_ref, keys_ref, e_ref, g_ref, *, tt):
    e_all, g_all, tie = _retrieve_block(q_ref, keys_ref, tt, _top16_no_ties)
    e_ref[...] = e_all
    g_ref[...] = g_all.T

    @pl.when(jnp.max(jnp.where(tie, 1.0, 0.0)) > 0.0)
    def _():
        e_x, g_x, _ = _retrieve_block(q_ref, keys_ref, tt, _top16)
        e_ref[...] = e_x
        g_ref[...] = g_x.T


def _retrieve(q, keys16):
    n, qw = q.shape
    tt = _blk(n, 128)
    return pl.pallas_call(
        functools.partial(_retrieve_kernel, tt=tt),
        grid=(n // tt,),
        in_specs=[pl.BlockSpec((tt, qw), lambda i: (i, 0)),
                  pl.BlockSpec(keys16.shape, lambda i: (0, 0, 0))],
        out_specs=[pl.BlockSpec((HK, tt), lambda i: (i, 0)),
                   pl.BlockSpec((tt, HK), lambda i: (i, 0))],
        out_shape=[jax.ShapeDtypeStruct((n // tt * HK, tt), jnp.int32),
                   jax.ShapeDtypeStruct((n, HK), F32)],
        compiler_params=pltpu.CompilerParams(
            dimension_semantics=("parallel",), vmem_limit_bytes=VMEM_LIMIT),
        name="peer_retrieve",
    )(q, keys16)


ROW_TILE = 8
GATHER_GROUP = 16
UP_VPU_PAIRS = 32


def _pack_table(tbl):
    e = tbl.shape[0]
    u = lax.bitcast_convert_type(tbl.astype(BF16), jnp.uint16).astype(jnp.uint32).reshape(e, 4, 2, 128)
    w = u[:, :, 0, :] | (u[:, :, 1, :] << 16)
    return lax.bitcast_convert_type(w, jnp.int32).reshape(e * 4, 128)


def _expert_row(tbl_ref, e):
    return tbl_ref[pl.ds(pl.multiple_of(e, 4), 4), :]


def _gather_rows(read_idx, tbl_ref, slab_ref, t, first, count):
    for m in range(0, count, 2):
        pair = [_expert_row(tbl_ref, read_idx(t, first + m + k)) for k in range(2)]
        slab_ref[m * 4:(m + 2) * 4, :] = jnp.concatenate(pair, axis=0)


def _with_index_block(idx_hbm, bufs, sems, tb, body):
    i = pl.program_id(0)
    n_steps = pl.num_programs(0)
    blk = HK * tb

    def copy(step, slot):
        return pltpu.make_async_copy(idx_hbm.at[pl.ds(pl.multiple_of(step * blk, blk), blk)], bufs[slot], sems.at[slot])

    @pl.when(i == 0)
    def _():
        copy(0, 0).start()

    for slot in range(2):
        @pl.when(i % 2 == slot)
        def _(slot=slot):
            copy(i, slot).wait()

            @pl.when(i + 1 < n_steps)
            def _():
                copy(i + 1, 1 - slot).start()

            buf = bufs[slot]
            body(lambda t, hk: buf.at[pl.ds(hk * tb, tb)][t])


def _split_bf16(x):
    hi = x.astype(BF16)
    lo = (x - hi.astype(F32)).astype(BF16)
    return hi, lo


def _diag_mask(rows, pairs):
    r = lax.broadcasted_iota(jnp.int32, (rows, pairs * ROW_TILE), 0)
    c = lax.broadcasted_iota(jnp.int32, (rows, pairs * ROW_TILE), 1)
    return jnp.bitwise_and(c, ROW_TILE - 1) == jnp.bitwise_and(r, ROW_TILE - 1)


def _pipelined_tokens(slabs, gather, tail, tb):
    group = min(GATHER_GROUP, tb)
    slabs[1][...] = jnp.zeros_like(slabs[1])

    def body(i, carry):
        t0 = group * i
        for j in range(group):
            gather(slabs[j % 2], t0 + j)
            tail(slabs[(j + 1) % 2], jnp.maximum(t0 + j - 1, 0))
        return carry

    lax.fori_loop(0, tb // group, body, 0)
    tail(slabs[1], tb - 1)


def _down_kernel(idx_hbm, x_ref, gate_ref, tbl_ref, sel_ref, rep_ref, chi8_ref, clo8_ref, chi_ref, clo_ref,
                 slab_a, slab_b, rows_ref, idx_a, idx_b, sems, *, tb):
    mask = _diag_mask(ROW_TILE, HK)

    def tail(slab_ref, t):
        hi, lo = _split_bf16(x_ref[t])
        g = _nt(jnp.concatenate([hi, lo], axis=0), pltpu.bitcast(slab_ref[...], BF16))
        gm = jnp.where(mask, g[0:ROW_TILE] + g[ROW_TILE:], 0.0)
        rows_ref[pl.ds(t, 1), :] = jnp.sum(gm, axis=0, keepdims=True)

    def body(read_idx):
        def gather(slab_ref, t):
            _gather_rows(read_idx, tbl_ref, slab_ref, t, 0, HK)

        _pipelined_tokens((slab_a, slab_b), gather, tail, tb)

    _with_index_block(idx_hbm, (idx_a, idx_b), sems, tb, body)
    hid = jnp.dot(rows_ref[...], sel_ref[...], precision=lax.Precision.HIGHEST, preferred_element_type=F32)
    c = gate_ref[...] * (0.5 * hid * (1.0 + lax.erf(hid * (2.0 ** -0.5))))
    chi, clo = _split_bf16(c)
    chi_ref[...] = chi.astype(F32)
    clo_ref[...] = clo.astype(F32)
    chi8_ref[...] = jnp.dot(chi, rep_ref[...], preferred_element_type=F32)
    clo8_ref[...] = jnp.dot(clo, rep_ref[...], preferred_element_type=F32)


def _peer_down(idx, x3, gate, tbl_packed, sel, rep):
    n = x3.shape[0]
    tb = _blk(n, 128)
    assert tb % 2 == 0
    wide = HK * ROW_TILE
    row_spec = pl.BlockSpec((tb, HK), lambda i: (i, 0))
    wide_spec = pl.BlockSpec((tb, wide), lambda i: (i, 0))
    return pl.pallas_call(
        functools.partial(_down_kernel, tb=tb),
        grid=(n // tb,),
        in_specs=[
            pl.BlockSpec(memory_space=pl.ANY),
            pl.BlockSpec((tb, ROW_TILE, 128), lambda i: (i, 0, 0)),
            row_spec,
            pl.BlockSpec(memory_space=pltpu.VMEM),
            pl.BlockSpec(memory_space=pltpu.VMEM),
            pl.BlockSpec(memory_space=pltpu.VMEM),
        ],
        out_specs=[wide_spec, wide_spec, row_spec, row_spec],
        out_shape=[jax.ShapeDtypeStruct((n, wide), F32), jax.ShapeDtypeStruct((n, wide), F32),
                   jax.ShapeDtypeStruct((n, HK), F32), jax.ShapeDtypeStruct((n, HK), F32)],
        scratch_shapes=[pltpu.VMEM((HK * 4, 128), jnp.int32), pltpu.VMEM((HK * 4, 128), jnp.int32),
                        pltpu.VMEM((tb, wide), F32),
                        pltpu.SMEM((HK * tb,), jnp.int32), pltpu.SMEM((HK * tb,), jnp.int32),
                        pltpu.SemaphoreType.DMA((2,))],
        compiler_params=pltpu.CompilerParams(
            dimension_semantics=("arbitrary",), vmem_limit_bytes=VMEM_LIMIT),
        name="peer_down",
    )(idx, x3, gate, tbl_packed, sel, rep)


def _up_kernel(idx_hbm, chi8_ref, clo8_ref, chi_ref, clo_ref, h_ref, g_ref, tbl_ref, out_ref,
               slab_a, slab_b, splat_ref, idx_a, idx_b, sems, *, tb):
    nv = UP_VPU_PAIRS
    nm = HK - nv
    wide = nm * ROW_TILE
    mask = _diag_mask(2 * ROW_TILE, nm)
    eye = (lax.broadcasted_iota(jnp.int32, (nv, HK), 0) == lax.broadcasted_iota(jnp.int32, (nv, HK), 1))
    ones = jnp.ones((2 * HK, 128), BF16)

    def vpu_pairs(read_idx, t):
        dh = jnp.where(eye, chi_ref[pl.ds(t, 1), :], 0.0)
        dl = jnp.where(eye, clo_ref[pl.ds(t, 1), :], 0.0)
        splat_ref[...] = jnp.dot(jnp.concatenate([dh, dl], axis=1).astype(BF16), ones, preferred_element_type=F32)
        accs = [jnp.zeros((4, 128), F32) for _ in range(4)]
        for hk in range(nv):
            w = _expert_row(tbl_ref, read_idx(t, hk))
            even = lax.bitcast_convert_type(w << 16, F32)
            odd = lax.bitcast_convert_type(w & jnp.int32(-65536), F32)
            cs = jnp.broadcast_to(splat_ref[hk:hk + 1, :], (4, 128))
            k = (hk % 2) * 2
            accs[k] = accs[k] + cs * even
            accs[k + 1] = accs[k + 1] + cs * odd
        out_ref[t, pl.ds(0, 4, stride=2), :] = accs[0] + accs[2]
        out_ref[t, pl.ds(1, 4, stride=2), :] = accs[1] + accs[3]

    def tail(slab_ref, t):
        hi = jnp.broadcast_to(chi8_ref[pl.ds(t, 1), nv * ROW_TILE:], (ROW_TILE, wide))
        lo = jnp.broadcast_to(clo8_ref[pl.ds(t, 1), nv * ROW_TILE:], (ROW_TILE, wide))
        lhs = jnp.where(mask, jnp.concatenate([hi, lo], axis=0), 0.0).astype(BF16)
        o = jnp.dot(lhs, pltpu.bitcast(slab_ref[...], BF16), preferred_element_type=F32)
        out_ref[t] = out_ref[t] + (o[0:ROW_TILE] + o[ROW_TILE:])

    def body(read_idx):
        def gather(slab_ref, t):
            _gather_rows(read_idx, tbl_ref, slab_ref, t, nv, nm)
            vpu_pairs(read_idx, t)

        _pipelined_tokens((slab_a, slab_b), gather, tail, tb)

    _with_index_block(idx_hbm, (idx_a, idx_b), sems, tb, body)
    v = out_ref[...] + h_ref[...]
    ms = jnp.sum(jnp.sum(v * v, axis=2, keepdims=True), axis=1, keepdims=True) * (1.0 / (ROW_TILE * 128))
    out_ref[...] = v * lax.rsqrt(ms + EPS) * g_ref[...]


def _peer_up(idx, chi8, clo8, chi, clo, h3, gain_tile, tbl_packed):
    n = h3.shape[0]
    tb = _blk(n, 128)
    assert tb % 2 == 0 and UP_VPU_PAIRS % ROW_TILE == 0
    nm = HK - UP_VPU_PAIRS
    row_spec = pl.BlockSpec((tb, HK), lambda i: (i, 0))
    wide_spec = pl.BlockSpec((tb, HK * ROW_TILE), lambda i: (i, 0))
    return pl.pallas_call(
        functools.partial(_up_kernel, tb=tb),
        grid=(n // tb,),
        in_specs=[
            pl.BlockSpec(memory_space=pl.ANY),
            wide_spec, wide_spec, row_spec, row_spec,
            pl.BlockSpec((tb, ROW_TILE, 128), lambda i: (i, 0, 0)),
            pl.BlockSpec((ROW_TILE, 128), lambda i: (0, 0)),
            pl.BlockSpec(memory_space=pltpu.VMEM),
        ],
        out_specs=pl.BlockSpec((tb, ROW_TILE, 128), lambda i: (i, 0, 0)),
        out_shape=jax.ShapeDtypeStruct((n, ROW_TILE, 128), F32),
        scratch_shapes=[pltpu.VMEM((nm * 4, 128), jnp.int32), pltpu.VMEM((nm * 4, 128), jnp.int32),
                        pltpu.VMEM((UP_VPU_PAIRS, 128), F32),
                        pltpu.SMEM((HK * tb,), jnp.int32), pltpu.SMEM((HK * tb,), jnp.int32),
                        pltpu.SemaphoreType.DMA((2,))],
        compiler_params=pltpu.CompilerParams(
            dimension_semantics=("arbitrary",), vmem_limit_bytes=VMEM_LIMIT),
        name="peer_up",
    )(idx, chi8, clo8, chi, clo, h3, gain_tile, tbl_packed)


def _trunk(x, past_k, past_v, past_s, prm):
    b, t, d = x.shape
    n = b * t
    lam_init = 0.8 - 0.6 * math.exp(-0.3 * 0)
    x2 = x.reshape(n, d)
    z, k_flat, v_flat = _inproj(x2, prm["norm1"], prm["w_in"])
    z4 = z.reshape(N_IN_BLOCKS, b, t, IN_BLOCK_W)
    if past_k is None:
        oa = _attn_prompt(z4, prm["lam"], prm["a_subln"], lam_init)
        s0 = jnp.zeros((b, R_HEADS, HEAD_W, HEAD_W), F32)
    else:
        p = past_k.shape[1]
        oa = _attn_sample(z4, past_k.reshape(b, p, A_HEADS * HEAD_W), past_v.reshape(b, p, A_HEADS * HEAD_W),
                          prm["lam"], prm["a_subln"], lam_init)
        s0 = past_s.astype(F32)
    orr, s_new = _hgrn(z4, prm["r_lb_logits"], prm["r_gnorm"], s0)
    h1, xn2, q = _merge(oa.reshape(n, -1), orr.reshape(n, -1), z, x2,
                        prm["w_a"], prm["w_b"], prm["w_out"], prm["norm2"], prm["p_wq"])
    eidx, gate = _retrieve(q, prm["p_keys"])
    eidx = eidx.reshape(-1)
    chi8, clo8, chi, clo = _peer_down(eidx, xn2.reshape(n, ROW_TILE, 128), gate,
                                      prm["down_packed"], prm["sel"], prm["rep"])
    y = _peer_up(eidx, chi8, clo8, chi, clo, h1.reshape(n, ROW_TILE, 128), prm["final_norm"],
                 prm["up_packed"]).reshape(b, t, d)
    k_new = k_flat.reshape(1, b, t, A_HEADS, HEAD_W)
    v_new = v_flat.reshape(1, b, t, A_HEADS, HEAD_W)
    return y, k_new, v_new, s_new[None].astype(x.dtype)


def kernel(x_prompt, x_sample, cache_k, cache_v, state_hgrn, norm1, w_in, lam_params, a_subln, r_lb_logits, r_gnorm, w_a, w_b, w_out, norm2, p_wq, p_keys, p_down, p_up, final_norm):
    assert w_in.shape[0] == 1 and x_prompt.shape[-1] == 8 * 128
    sel = (lax.broadcasted_iota(jnp.int32, (HK * ROW_TILE, HK), 0) // ROW_TILE
           == lax.broadcasted_iota(jnp.int32, (HK * ROW_TILE, HK), 1)).astype(F32)
    prm = {
        "norm1": norm1[0][None].astype(F32),
        "w_in": w_in[0].astype(BF16),
        "lam": lam_params[0].astype(F32),
        "a_subln": a_subln[0][None].astype(F32),
        "r_lb_logits": r_lb_logits.astype(F32),
        "r_gnorm": r_gnorm[0][None].astype(F32),
        "w_a": w_a[0].astype(BF16),
        "w_b": w_b[0].astype(BF16),
        "w_out": w_out[0].astype(BF16),
        "norm2": norm2[0][None].astype(F32),
        "p_wq": p_wq[0].astype(BF16),
        "p_keys": p_keys[0].reshape(P_HEADS * 2, P_NKEYS, -1).astype(F32),
        "down_packed": _pack_table(p_down[0]),
        "up_packed": _pack_table(p_up[0]),
        "sel": sel,
        "rep": sel.T.astype(BF16),
        "final_norm": final_norm.reshape(ROW_TILE, 128).astype(F32),
    }
    y_p, k_p, v_p, s_p = _trunk(x_prompt, None, None, None, prm)
    y_s, k_s, v_s, s_s = _trunk(x_sample, cache_k[0], cache_v[0], state_hgrn[0], prm)
    return (y_p, y_s, k_p, v_p, s_p, k_s, v_s, s_s)
```

```python
import functools
import math

import jax
import jax.numpy as jnp
from jax import lax
from jax.experimental import pallas as pl
from jax.experimental.pallas import tpu as pltpu

F32 = jnp.float32
BF16 = jnp.bfloat16
EPS = 1e-6
CHUNK = 64
A_HEADS = 4
A_DK = 64
R_HEADS = 4
R_BLOCK = 16
P_HEADS = 8
P_NKEYS = 128
P_TOPK = 16
HK = P_HEADS * P_TOPK
HEAD_W = 128
N_IN_BLOCKS = 11
IN_BLOCK_W = 512
VMEM_LIMIT = 52 * 1024 * 1024


def _blk(n, pref):
    if n <= pref:
        return n
    b = pref
    while n % b:
        b //= 2
    assert b >= 8, (n, pref)
    return b


def _nt(a, b):
    return lax.dot_general(a, b, (((1,), (1,)), ((), ())), preferred_element_type=F32)


def _rms(x, gain):
    return x * lax.rsqrt(jnp.mean(x * x, axis=-1, keepdims=True) + EPS) * gain


K_BLOCK, V_BLOCK = 1, 2


def _inproj_kernel(x_ref, g_ref, w_ref, o_ref, k_ref, v_ref, xn_ref):
    j = pl.program_id(1)

    @pl.when(j == 0)
    def _():
        xn_ref[...] = _rms(x_ref[...], g_ref[...]).astype(BF16)

    r = jnp.dot(xn_ref[...], w_ref[...], preferred_element_type=F32)
    o_ref[...] = r

    @pl.when(j == K_BLOCK)
    def _():
        k_ref[...] = r

    @pl.when(j == V_BLOCK)
    def _():
        v_ref[...] = r


def _inproj(x, gain, w_bf16):
    n, d = x.shape
    tm = _blk(n, 2048)
    kv_spec = pl.BlockSpec((tm, IN_BLOCK_W), lambda i, j: (i, 0), pipeline_mode=pl.Buffered(1))
    return pl.pallas_call(
        _inproj_kernel,
        grid=(n // tm, N_IN_BLOCKS),
        in_specs=[
            pl.BlockSpec((tm, d), lambda i, j: (i, 0)),
            pl.BlockSpec((1, d), lambda i, j: (0, 0)),
            pl.BlockSpec((d, IN_BLOCK_W), lambda i, j: (0, j)),
        ],
        out_specs=[pl.BlockSpec((None, tm, IN_BLOCK_W), lambda i, j: (j, i, 0)), kv_spec, kv_spec],
        out_shape=[jax.ShapeDtypeStruct((N_IN_BLOCKS, n, IN_BLOCK_W), F32),
                   jax.ShapeDtypeStruct((n, IN_BLOCK_W), F32),
                   jax.ShapeDtypeStruct((n, IN_BLOCK_W), F32)],
        scratch_shapes=[pltpu.VMEM((tm, d), BF16)],
        compiler_params=pltpu.CompilerParams(
            dimension_semantics=("parallel", "arbitrary"), vmem_limit_bytes=VMEM_LIMIT),
        name="inproj",
    )(x, gain, w_bf16)


def _lam_from_params(lp, lam_init):
    a = jnp.sum(lp[0:1] * lp[1:2], axis=(0, 1), keepdims=True)
    b = jnp.sum(lp[2:3] * lp[3:4], axis=(0, 1), keepdims=True)
    return jnp.exp(a) - jnp.exp(b) + lam_init


def _map_masks():
    lane = lax.broadcasted_iota(jnp.int32, (1, HEAD_W), 1)
    m1 = (lane < A_DK).astype(F32)
    return m1, 1.0 - m1


def _chunk_id(pos):
    return lax.shift_right_logical(pos, int(math.log2(CHUNK)))


def _attn_prompt_kernel(lam_ref, q_ref, k_ref, v_ref, sub_ref, o_ref, *, seq, qb, lam_init):
    lam = _lam_from_params(lam_ref[...], lam_init)
    m1, m2 = _map_masks()
    kb = k_ref[...].astype(BF16)
    vb = v_ref[...].astype(BF16)
    gain = sub_ref[...] * (1.0 - lam_init)
    for j in range(seq // qb):
        kv_len = (j + 1) * qb
        q = q_ref[j * qb:(j + 1) * qb, :] * (A_DK ** -0.5)
        kk = kb[0:kv_len]
        s1 = _nt((q * m1).astype(BF16), kk)
        s2 = _nt((q * m2).astype(BF16), kk)
        qpos = j * qb + lax.broadcasted_iota(jnp.int32, (qb, kv_len), 0)
        kpos = lax.broadcasted_iota(jnp.int32, (qb, kv_len), 1)
        mask = _chunk_id(kpos) <= _chunk_id(qpos)
        s1 = jnp.where(mask, s1, -jnp.inf)
        s2 = jnp.where(mask, s2, -jnp.inf)
        e1 = jnp.exp(s1 - jnp.max(s1, axis=-1, keepdims=True))
        e2 = jnp.exp(s2 - jnp.max(s2, axis=-1, keepdims=True))
        r1 = 1.0 / jnp.sum(e1, axis=-1, keepdims=True)
        r2 = lam / jnp.sum(e2, axis=-1, keepdims=True)
        w = (e1 * r1 - e2 * r2).astype(BF16)
        o = jnp.dot(w, vb[0:kv_len], preferred_element_type=F32)
        o_ref[j * qb:(j + 1) * qb, :] = _rms(o, gain)


def _attn_prompt(z4, lam_params, a_subln, lam_init):
    _, b, t, _ = z4.shape
    qb = _blk(t, 256)

    def zspec(col):
        return pl.BlockSpec((None, None, t, HEAD_W), lambda bi, h, col=col: (col, bi, 0, h))

    return pl.pallas_call(
        functools.partial(_attn_prompt_kernel, seq=t, qb=qb, lam_init=lam_init),
        grid=(b, A_HEADS),
        in_specs=[
            pl.BlockSpec((4, A_DK), lambda bi, h: (0, 0)),
            zspec(0), zspec(1), zspec(2),
            pl.BlockSpec((1, HEAD_W), lambda bi, h: (0, 0)),
        ],
        out_specs=pl.BlockSpec((None, t, HEAD_W), lambda bi, h: (bi, 0, h)),
        out_shape=jax.ShapeDtypeStruct((b, t, A_HEADS * HEAD_W), F32),
        compiler_params=pltpu.CompilerParams(
            dimension_semantics=("parallel", "parallel"), vmem_limit_bytes=VMEM_LIMIT),
        name="attn_prompt",
    )(lam_params, z4, z4, z4, a_subln)


def _attn_sample_kernel(lam_ref, q_ref, k_ref, v_ref, pk_ref, pv_ref, sub_ref, o_ref, *, seq, past, lam_init):
    lam = _lam_from_params(lam_ref[...], lam_init)
    m1, m2 = _map_masks()
    gain = sub_ref[...] * (1.0 - lam_init)
    q = q_ref[...] * (A_DK ** -0.5)
    kn = k_ref[...].astype(BF16)
    kp = pk_ref[...].astype(BF16)
    qpos_p = past + lax.broadcasted_iota(jnp.int32, (seq, past), 0)
    kpos_p = lax.broadcasted_iota(jnp.int32, (seq, past), 1)
    mask_p = _chunk_id(kpos_p) <= _chunk_id(qpos_p)
    qpos_n = past + lax.broadcasted_iota(jnp.int32, (seq, seq), 0)
    kpos_n = past + lax.broadcasted_iota(jnp.int32, (seq, seq), 1)
    mask_n = _chunk_id(kpos_n) <= _chunk_id(qpos_n)

    def one_map(qm):
        sp = jnp.where(mask_p, _nt(qm, kp), -jnp.inf)
        sn = jnp.where(mask_n, _nt(qm, kn), -jnp.inf)
        mx = jnp.maximum(jnp.max(sp, axis=-1, keepdims=True), jnp.max(sn, axis=-1, keepdims=True))
        ep = jnp.exp(sp - mx)
        en = jnp.exp(sn - mx)
        tot = jnp.sum(ep, axis=-1, keepdims=True) + jnp.sum(en, axis=-1, keepdims=True)
        return ep, en, tot

    ep1, en1, t1 = one_map((q * m1).astype(BF16))
    ep2, en2, t2 = one_map((q * m2).astype(BF16))
    r1 = 1.0 / t1
    r2 = lam / t2
    wp = (ep1 * r1 - ep2 * r2).astype(BF16)
    wn = (en1 * r1 - en2 * r2).astype(BF16)
    o = (jnp.dot(wp, pv_ref[...].astype(BF16), preferred_element_type=F32)
         + jnp.dot(wn, v_ref[...].astype(BF16), preferred_element_type=F32))
    o_ref[...] = _rms(o, gain)


def _attn_sample(z4, past_k, past_v, lam_params, a_subln, lam_init):
    _, b, t, _ = z4.shape
    p = past_k.shape[1]

    def zspec(col):
        return pl.BlockSpec((None, None, t, HEAD_W), lambda bi, h, col=col: (col, bi, 0, h))

    pspec = pl.BlockSpec((None, p, HEAD_W), lambda bi, h: (bi, 0, h))
    return pl.pallas_call(
        functools.partial(_attn_sample_kernel, seq=t, past=p, lam_init=lam_init),
        grid=(b, A_HEADS),
        in_specs=[
            pl.BlockSpec((4, A_DK), lambda bi, h: (0, 0)),
            zspec(0), zspec(1), zspec(2), pspec, pspec,
            pl.BlockSpec((1, HEAD_W), lambda bi, h: (0, 0)),
        ],
        out_specs=pl.BlockSpec((None, t, HEAD_W), lambda bi, h: (bi, 0, h)),
        out_shape=jax.ShapeDtypeStruct((b, t, A_HEADS * HEAD_W), F32),
        compiler_params=pltpu.CompilerParams(
            dimension_semantics=("parallel", "parallel"), vmem_limit_bytes=VMEM_LIMIT),
        name="attn_sample",
    )(lam_params, z4, z4, z4, past_k, past_v, a_subln)


R_HEADS_PER_STEP = 2


def _hgrn_kernel(f_ref, q_ref, i_ref, og_ref, lbl_ref, gn_ref, s0_ref, o_ref, s_ref, st_ref, *, seq, ch):
    nb = ch // R_BLOCK
    shift = int(math.log2(R_BLOCK))
    logits = lbl_ref[...]
    ex = jnp.exp(logits - jnp.max(logits, axis=0, keepdims=True))
    lb_all = ex[0:1] / jnp.sum(ex, axis=0, keepdims=True)
    row = lax.broadcasted_iota(jnp.int32, (ch, ch), 0)
    col = lax.broadcasted_iota(jnp.int32, (ch, ch), 1)
    same = lax.shift_right_logical(row, shift) == lax.shift_right_logical(col, shift)
    causal = jnp.logical_and(same, col <= row)
    sum_m = jnp.concatenate([causal, same], axis=0).astype(BF16)
    gn = gn_ref[...]
    for hh in range(R_HEADS_PER_STEP):
        st_ref[hh] = s0_ref[hh].T

    def one_head(hh, sl):
        cs = slice(hh * HEAD_W, (hh + 1) * HEAD_W)
        lb = lb_all[:, cs]
        g = lb + (1.0 - lb) * jax.nn.sigmoid(f_ref[sl, cs])
        logf = jnp.log(g)
        kk = 1.0 - g
        qp = q_ref[sl, cs]
        qq = qp * jax.nn.sigmoid(qp)
        vv = i_ref[sl, cs]
        p0 = logf.astype(BF16)
        r0 = logf - p0.astype(F32)
        p1 = r0.astype(BF16)
        p2 = (r0 - p1.astype(F32)).astype(BF16)
        sums = jnp.dot(sum_m, jnp.concatenate([p0, p1, p2], axis=1), preferred_element_type=F32)
        sums = sums[:, 0:HEAD_W] + sums[:, HEAD_W:2 * HEAD_W] + sums[:, 2 * HEAD_W:]
        b = sums[0:ch]
        bl = sums[ch:]
        q_in = (qq * jnp.exp(b)).astype(BF16)
        k_in = (kk * jnp.exp(-b)).astype(BF16)
        k_out = (kk * jnp.exp(bl - b)).astype(BF16)
        vb = vv.astype(BF16)
        a = jnp.where(causal, _nt(q_in, k_in), 0.0)
        o_intra = jnp.dot(a.astype(BF16), vb, preferred_element_type=F32)
        upds = [lax.dot_general(vb[r0:r0 + R_BLOCK], k_out[r0:r0 + R_BLOCK],
                                (((0,), (0,)), ((), ())), preferred_element_type=F32)
                for r0 in range(0, ch, R_BLOCK)]
        sts = [st_ref[hh]]
        for blk in range(nb):
            dl = jnp.exp(bl[blk * R_BLOCK:blk * R_BLOCK + 1])
            sts.append(sts[-1] * dl + upds[blk])
        st_ref[hh] = sts[nb]
        inter = [_nt(q_in[blk * R_BLOCK:(blk + 1) * R_BLOCK], sts[blk].astype(BF16))
                 for blk in range(nb)]
        o = o_intra + jnp.concatenate(inter, axis=0)
        ogp = og_ref[sl, cs]
        o_ref[sl, cs] = _rms(o, gn) * (ogp * jax.nn.sigmoid(ogp))

    def chunk(c, carry):
        sl = pl.ds(pl.multiple_of(c * ch, ch), ch)
        for hh in range(R_HEADS_PER_STEP):
            one_head(hh, sl)
        return carry

    lax.fori_loop(0, seq // ch, chunk, 0)
    for hh in range(R_HEADS_PER_STEP):
        s_ref[hh] = st_ref[hh].T


def _hgrn(z4, r_lb_logits, r_gnorm, s0):
    _, b, t, _ = z4.shape
    assert t % R_BLOCK == 0 and R_HEADS % R_HEADS_PER_STEP == 0
    ch = _blk(t, 128)
    nl = r_lb_logits.shape[0]
    hw = R_HEADS_PER_STEP * HEAD_W

    def zspec(col):
        return pl.BlockSpec((None, None, t, hw), lambda bi, h, col=col: (col, bi, 0, h))

    sspec = pl.BlockSpec((None, R_HEADS_PER_STEP, HEAD_W, HEAD_W), lambda bi, h: (bi, h, 0, 0))
    return pl.pallas_call(
        functools.partial(_hgrn_kernel, seq=t, ch=ch),
        grid=(b, R_HEADS // R_HEADS_PER_STEP),
        in_specs=[
            zspec(3), zspec(4), zspec(5), zspec(6),
            pl.BlockSpec((nl, hw), lambda bi, h: (0, h)),
            pl.BlockSpec((1, HEAD_W), lambda bi, h: (0, 0)),
            sspec,
        ],
        out_specs=[pl.BlockSpec((None, t, hw), lambda bi, h: (bi, 0, h)), sspec],
        out_shape=[jax.ShapeDtypeStruct((b, t, R_HEADS * HEAD_W), F32),
                   jax.ShapeDtypeStruct((b, R_HEADS, HEAD_W, HEAD_W), F32)],
        scratch_shapes=[pltpu.VMEM((R_HEADS_PER_STEP, HEAD_W, HEAD_W), F32)],
        compiler_params=pltpu.CompilerParams(
            dimension_semantics=("parallel", "parallel"), vmem_limit_bytes=VMEM_LIMIT),
        name="hgrn2",
    )(z4, z4, z4, z4, r_lb_logits, r_gnorm, s0)


def _merge_kernel(oa_ref, or_ref, ga0_ref, ga1_ref, gb0_ref, gb1_ref, x_ref,
                  wa_ref, wb_ref, wo_ref, n2_ref, wq_ref, h_ref, xn_ref, q_ref):
    pa = jnp.dot(oa_ref[...].astype(BF16), wa_ref[...], preferred_element_type=F32)
    pb = jnp.dot(or_ref[...].astype(BF16), wb_ref[...], preferred_element_type=F32)
    ga = jnp.concatenate([ga0_ref[...], ga1_ref[...]], axis=-1)
    gb = jnp.concatenate([gb0_ref[...], gb1_ref[...]], axis=-1)
    m = jax.nn.sigmoid(ga) * pa + jax.nn.sigmoid(gb) * pb
    h = x_ref[...] + jnp.dot(m.astype(BF16), wo_ref[...], preferred_element_type=F32)
    h_ref[...] = h
    xn = _rms(h, n2_ref[...])
    xn_ref[...] = xn
    q_ref[...] = jnp.dot(xn.astype(BF16), wq_ref[...], preferred_element_type=F32)


def _merge(oa, orr, z, x, w_a, w_b, w_out, norm2, p_wq):
    n, d = x.shape
    tm = _blk(n, 512)
    qw = p_wq.shape[1]

    def zspec(col):
        return pl.BlockSpec((None, tm, IN_BLOCK_W), lambda i, col=col: (col, i, 0))

    def full(a):
        return pl.BlockSpec(a.shape, lambda i: (0,) * a.ndim, pipeline_mode=pl.Buffered(1))

    return pl.pallas_call(
        _merge_kernel,
        grid=(n // tm,),
        in_specs=[
            pl.BlockSpec((tm, oa.shape[1]), lambda i: (i, 0)),
            pl.BlockSpec((tm, orr.shape[1]), lambda i: (i, 0)),
            zspec(7), zspec(8), zspec(9), zspec(10),
            pl.BlockSpec((tm, d), lambda i: (i, 0)),
            full(w_a), full(w_b), full(w_out), full(norm2), full(p_wq),
        ],
        out_specs=[pl.BlockSpec((tm, d), lambda i: (i, 0)),
                   pl.BlockSpec((tm, d), lambda i: (i, 0)),
                   pl.BlockSpec((tm, qw), lambda i: (i, 0))],
        out_shape=[jax.ShapeDtypeStruct((n, d), F32),
                   jax.ShapeDtypeStruct((n, d), F32),
                   jax.ShapeDtypeStruct((n, qw), F32)],
        compiler_params=pltpu.CompilerParams(
            dimension_semantics=("parallel",), vmem_limit_bytes=VMEM_LIMIT),
        name="merge_proj",
    )(oa, orr, z, z, z, z, x, w_a, w_b, w_out, norm2, p_wq)


TIE_SHIFT = 1024.0


def _top16(s, order, sentinel):
    vals, picks = [], []
    for _ in range(P_TOPK):
        m = jnp.max(s, axis=0, keepdims=True)
        am = jnp.min(jnp.where(s == m, order, sentinel), axis=0, keepdims=True)
        vals.append(m)
        picks.append(am)
        s = jnp.where(order == am, -jnp.inf, s)
    return vals, picks, None


def _top16_no_ties(s, order, sentinel):
    del sentinel
    vals, keys = [], []
    for _ in range(P_TOPK):
        m = jnp.max(s, axis=0, keepdims=True)
        eq = s == m
        keys.append(jnp.sum(jnp.where(eq, order + TIE_SHIFT, 0.0), axis=0, keepdims=True))
        vals.append(m)
        s = jnp.where(eq, -jnp.inf, s)
    worst = functools.reduce(jnp.maximum, keys)
    return vals, [k - TIE_SHIFT for k in keys], worst >= 2.0 * TIE_SHIFT


def _pair_candidates(tt):
    half = P_TOPK // 2
    j16 = lax.broadcasted_iota(jnp.int32, (P_TOPK, tt), 0)
    j8 = lax.broadcasted_iota(jnp.int32, (half, tt), 0)
    pieces = [j16] + [i * P_TOPK + j8 for i in range(1, half)] + [(half + j8) * P_TOPK]
    return jnp.concatenate(pieces, axis=0).astype(F32)


def _retrieve_block(q_ref, keys_ref, tt, top16):
    half = P_TOPK // 2
    iota16 = lax.broadcasted_iota(jnp.int32, (P_TOPK, tt), 0).astype(F32)
    key_iota = lax.broadcasted_iota(jnp.int32, (P_NKEYS, tt), 0).astype(F32)
    flat = _pair_candidates(tt)
    e_rows, g_rows, flags = [], [], []
    for h in range(P_HEADS):
        sv, si = [], []
        for c in range(2):
            hc = h * 2 + c
            s = _nt(keys_ref[hc], q_ref[:, hc * P_NKEYS:(hc + 1) * P_NKEYS])
            vals, idxs, flag = top16(s, key_iota, float(P_NKEYS))
            flags.append(flag)
            sv.append(jnp.concatenate(vals, axis=0))
            si.append(jnp.concatenate(idxs, axis=0))
        comb = jnp.concatenate(
            [sv[0][0:1] + sv[1]]
            + [sv[0][i:i + 1] + sv[1][0:half] for i in range(1, half)]
            + [sv[0][half:] + sv[1][0:1]], axis=0)
        cvals, cidx, flag = top16(comb, flat, float(P_TOPK * P_TOPK))
        flags.append(flag)
        for k in range(P_TOPK):
            i0 = jnp.floor(cidx[k] * (1.0 / P_TOPK))
            i1 = cidx[k] - i0 * P_TOPK
            e0 = jnp.sum(jnp.where(iota16 == i0, si[0], 0.0), axis=0, keepdims=True)
            e1 = jnp.sum(jnp.where(iota16 == i1, si[1], 0.0), axis=0, keepdims=True)
            e_rows.append(e0 * P_NKEYS + e1)
        cv = jnp.concatenate(cvals, axis=0)
        ex = jnp.exp(cv - cvals[0])
        g_rows.append(ex / jnp.sum(ex, axis=0, keepdims=True))
    e_all = (jnp.concatenate(e_rows, axis=0) * 4.0).astype(jnp.int32)
    g_all = jnp.concatenate(g_rows, axis=0)
    flags = [f for f in flags if f is not None]
    return e_all, g_all, (functools.reduce(jnp.logical_or, flags) if flags else None)


def _retrieve_kernel(q_ref, keys_ref, e_ref, g_ref, *, tt):
    e_all, g_all, tie = _retrieve_block(q_ref, keys_ref, tt, _top16_no_ties)
    e_ref[...] = e_all
    g_ref[...] = g_all.T

    @pl.when(jnp.max(jnp.where(tie, 1.0, 0.0)) > 0.0)
    def _():
        e_x, g_x, _ = _retrieve_block(q_ref, keys_ref, tt, _top16)
        e_ref[...] = e_x
        g_ref[...] = g_x.T


def _retrieve(q, keys16):
    n, qw = q.shape
    tt = _blk(n, 128)
    return pl.pallas_call(
        functools.partial(_retrieve_kernel, tt=tt),
        grid=(n // tt,),
        in_specs=[pl.BlockSpec((tt, qw), lambda i: (i, 0)),
                  pl.BlockSpec(keys16.shape, lambda i: (0, 0, 0))],
        out_specs=[pl.BlockSpec((HK, tt), lambda i: (i, 0)),
                   pl.BlockSpec((tt, HK), lambda i: (i, 0))],
        out_shape=[jax.ShapeDtypeStruct((n // tt * HK, tt), jnp.int32),
                   jax.ShapeDtypeStruct((n, HK), F32)],
        compiler_params=pltpu.CompilerParams(
            dimension_semantics=("parallel",), vmem_limit_bytes=VMEM_LIMIT),
        name="peer_retrieve",
    )(q, keys16)


ROW_TILE = 8
GATHER_GROUP = 32
UP_VPU_PAIRS = 32


def _pack_table(tbl):
    e = tbl.shape[0]
    u = lax.bitcast_convert_type(tbl.astype(BF16), jnp.uint16).astype(jnp.uint32).reshape(e, 4, 2, 128)
    w = u[:, :, 0, :] | (u[:, :, 1, :] << 16)
    return lax.bitcast_convert_type(w, jnp.int32).reshape(e * 4, 128)


def _expert_row(tbl_ref, e):
    return tbl_ref[pl.ds(pl.multiple_of(e, 4), 4), :]


def _gather_rows(read_idx, tbl_ref, slab_ref, t, first, count):
    for m in range(0, count, 2):
        pair = [_expert_row(tbl_ref, read_idx(t, first + m + k)) for k in range(2)]
        slab_ref[m * 4:(m + 2) * 4, :] = jnp.concatenate(pair, axis=0)


def _with_index_block(idx_hbm, bufs, sems, tb, body):
    i = pl.program_id(0)
    n_steps = pl.num_programs(0)
    blk = HK * tb

    def copy(step, slot):
        return pltpu.make_async_copy(idx_hbm.at[pl.ds(pl.multiple_of(step * blk, blk), blk)], bufs[slot], sems.at[slot])

    @pl.when(i == 0)
    def _():
        copy(0, 0).start()

    for slot in range(2):
        @pl.when(i % 2 == slot)
        def _(slot=slot):
            copy(i, slot).wait()

            @pl.when(i + 1 < n_steps)
            def _():
                copy(i + 1, 1 - slot).start()

            buf = bufs[slot]
            body(lambda t, hk: buf.at[pl.ds(hk * tb, tb)][t])


def _split_bf16(x):
    hi = x.astype(BF16)
    lo = (x - hi.astype(F32)).astype(BF16)
    return hi, lo


def _diag_mask(rows, pairs):
    r = lax.broadcasted_iota(jnp.int32, (rows, pairs * ROW_TILE), 0)
    c = lax.broadcasted_iota(jnp.int32, (rows, pairs * ROW_TILE), 1)
    return jnp.bitwise_and(c, ROW_TILE - 1) == jnp.bitwise_and(r, ROW_TILE - 1)


def _pipelined_tokens(slabs, gather, tail, tb):
    group = min(GATHER_GROUP, tb)
    slabs[1][...] = jnp.zeros_like(slabs[1])

    def body(i, carry):
        t0 = group * i
        for j in range(group):
            gather(slabs[j % 2], t0 + j)
            tail(slabs[(j + 1) % 2], jnp.maximum(t0 + j - 1, 0))
        return carry

    lax.fori_loop(0, tb // group, body, 0)
    tail(slabs[1], tb - 1)


def _down_kernel(idx_hbm, x_ref, gate_ref, tbl_ref, sel_ref, rep_ref, chi8_ref, clo8_ref, chi_ref, clo_ref,
                 slab_a, slab_b, rows_ref, idx_a, idx_b, sems, *, tb):
    mask = _diag_mask(ROW_TILE, HK)

    def tail(slab_ref, t):
        hi, lo = _split_bf16(x_ref[t])
        g = _nt(jnp.concatenate([hi, lo], axis=0), pltpu.bitcast(slab_ref[...], BF16))
        gm = jnp.where(mask, g[0:ROW_TILE] + g[ROW_TILE:], 0.0)
        rows_ref[pl.ds(t, 1), :] = jnp.sum(gm, axis=0, keepdims=True)

    def body(read_idx):
        def gather(slab_ref, t):
            _gather_rows(read_idx, tbl_ref, slab_ref, t, 0, HK)

        _pipelined_tokens((slab_a, slab_b), gather, tail, tb)

    _with_index_block(idx_hbm, (idx_a, idx_b), sems, tb, body)
    hid = jnp.dot(rows_ref[...], sel_ref[...], precision=lax.Precision.HIGHEST, preferred_element_type=F32)
    c = gate_ref[...] * (0.5 * hid * (1.0 + lax.erf(hid * (2.0 ** -0.5))))
    chi, clo = _split_bf16(c)
    chi_ref[...] = chi.astype(F32)
    clo_ref[...] = clo.astype(F32)
    chi8_ref[...] = jnp.dot(chi, rep_ref[...], preferred_element_type=F32)
    clo8_ref[...] = jnp.dot(clo, rep_ref[...], preferred_element_type=F32)


def _peer_down(idx, x3, gate, tbl_packed, sel, rep):
    n = x3.shape[0]
    tb = _blk(n, 128)
    assert tb % 2 == 0
    wide = HK * ROW_TILE
    row_spec = pl.BlockSpec((tb, HK), lambda i: (i, 0))
    wide_spec = pl.BlockSpec((tb, wide), lambda i: (i, 0))
    return pl.pallas_call(
        functools.partial(_down_kernel, tb=tb),
        grid=(n // tb,),
        in_specs=[
            pl.BlockSpec(memory_space=pl.ANY),
            pl.BlockSpec((tb, ROW_TILE, 128), lambda i: (i, 0, 0)),
            row_spec,
            pl.BlockSpec(memory_space=pltpu.VMEM),
            pl.BlockSpec(memory_space=pltpu.VMEM),
            pl.BlockSpec(memory_space=pltpu.VMEM),
        ],
        out_specs=[wide_spec, wide_spec, row_spec, row_spec],
        out_shape=[jax.ShapeDtypeStruct((n, wide), F32), jax.ShapeDtypeStruct((n, wide), F32),
                   jax.ShapeDtypeStruct((n, HK), F32), jax.ShapeDtypeStruct((n, HK), F32)],
        scratch_shapes=[pltpu.VMEM((HK * 4, 128), jnp.int32), pltpu.VMEM((HK * 4, 128), jnp.int32),
                        pltpu.VMEM((tb, wide), F32),
                        pltpu.SMEM((HK * tb,), jnp.int32), pltpu.SMEM((HK * tb,), jnp.int32),
                        pltpu.SemaphoreType.DMA((2,))],
        compiler_params=pltpu.CompilerParams(
            dimension_semantics=("arbitrary",), vmem_limit_bytes=VMEM_LIMIT),
        name="peer_down",
    )(idx, x3, gate, tbl_packed, sel, rep)


def _up_kernel(idx_hbm, chi8_ref, clo8_ref, chi_ref, clo_ref, h_ref, g_ref, tbl_ref, out_ref,
               slab_a, slab_b, splat_ref, idx_a, idx_b, sems, *, tb):
    nv = UP_VPU_PAIRS
    nm = HK - nv
    wide = nm * ROW_TILE
    mask = _diag_mask(2 * ROW_TILE, nm)
    eye = (lax.broadcasted_iota(jnp.int32, (nv, HK), 0) == lax.broadcasted_iota(jnp.int32, (nv, HK), 1))
    ones = jnp.ones((2 * HK, 128), BF16)

    def vpu_pairs(read_idx, t):
        dh = jnp.where(eye, chi_ref[pl.ds(t, 1), :], 0.0)
        dl = jnp.where(eye, clo_ref[pl.ds(t, 1), :], 0.0)
        splat_ref[...] = jnp.dot(jnp.concatenate([dh, dl], axis=1).astype(BF16), ones, preferred_element_type=F32)
        accs = [jnp.zeros((4, 128), F32) for _ in range(4)]
        for hk in range(nv):
            w = _expert_row(tbl_ref, read_idx(t, hk))
            even = lax.bitcast_convert_type(w << 16, F32)
            odd = lax.bitcast_convert_type(w & jnp.int32(-65536), F32)
            cs = jnp.broadcast_to(splat_ref[hk:hk + 1, :], (4, 128))
            k = (hk % 2) * 2
            accs[k] = accs[k] + cs * even
            accs[k + 1] = accs[k + 1] + cs * odd
        out_ref[t, pl.ds(0, 4, stride=2), :] = accs[0] + accs[2]
        out_ref[t, pl.ds(1, 4, stride=2), :] = accs[1] + accs[3]

    def tail(slab_ref, t):
        hi = jnp.broadcast_to(chi8_ref[pl.ds(t, 1), nv * ROW_TILE:], (ROW_TILE, wide))
        lo = jnp.broadcast_to(clo8_ref[pl.ds(t, 1), nv * ROW_TILE:], (ROW_TILE, wide))
        lhs = jnp.where(mask, jnp.concatenate([hi, lo], axis=0), 0.0).astype(BF16)
        o = jnp.dot(lhs, pltpu.bitcast(slab_ref[...], BF16), preferred_element_type=F32)
        out_ref[t] = out_ref[t] + (o[0:ROW_TILE] + o[ROW_TILE:])

    def body(read_idx):
        def gather(slab_ref, t):
            _gather_rows(read_idx, tbl_ref, slab_ref, t, nv, nm)
            vpu_pairs(read_idx, t)

        _pipelined_tokens((slab_a, slab_b), gather, tail, tb)

    _with_index_block(idx_hbm, (idx_a, idx_b), sems, tb, body)
    v = out_ref[...] + h_ref[...]
    ms = jnp.sum(jnp.sum(v * v, axis=2, keepdims=True), axis=1, keepdims=True) * (1.0 / (ROW_TILE * 128))
    out_ref[...] = v * lax.rsqrt(ms + EPS) * g_ref[...]


def _peer_up(idx, chi8, clo8, chi, clo, h3, gain_tile, tbl_packed):
    n = h3.shape[0]
    tb = _blk(n, 128)
    assert tb % 2 == 0 and UP_VPU_PAIRS % ROW_TILE == 0
    nm = HK - UP_VPU_PAIRS
    row_spec = pl.BlockSpec((tb, HK), lambda i: (i, 0))
    wide_spec = pl.BlockSpec((tb, HK * ROW_TILE), lambda i: (i, 0))
    return pl.pallas_call(
        functools.partial(_up_kernel, tb=tb),
        grid=(n // tb,),
        in_specs=[
            pl.BlockSpec(memory_space=pl.ANY),
            wide_spec, wide_spec, row_spec, row_spec,
            pl.BlockSpec((tb, ROW_TILE, 128), lambda i: (i, 0, 0)),
            pl.BlockSpec((ROW_TILE, 128), lambda i: (0, 0)),
            pl.BlockSpec(memory_space=pltpu.VMEM),
        ],
        out_specs=pl.BlockSpec((tb, ROW_TILE, 128), lambda i: (i, 0, 0)),
        out_shape=jax.ShapeDtypeStruct((n, ROW_TILE, 128), F32),
        scratch_shapes=[pltpu.VMEM((nm * 4, 128), jnp.int32), pltpu.VMEM((nm * 4, 128), jnp.int32),
                        pltpu.VMEM((UP_VPU_PAIRS, 128), F32),
                        pltpu.SMEM((HK * tb,), jnp.int32), pltpu.SMEM((HK * tb,), jnp.int32),
                        pltpu.SemaphoreType.DMA((2,))],
        compiler_params=pltpu.CompilerParams(
            dimension_semantics=("arbitrary",), vmem_limit_bytes=VMEM_LIMIT),
        name="peer_up",
    )(idx, chi8, clo8, chi, clo, h3, gain_tile, tbl_packed)


def _trunk(x, past_k, past_v, past_s, prm):
    b, t, d = x.shape
    n = b * t
    lam_init = 0.8 - 0.6 * math.exp(-0.3 * 0)
    x2 = x.reshape(n, d)
    z, k_flat, v_flat = _inproj(x2, prm["norm1"], prm["w_in"])
    z4 = z.reshape(N_IN_BLOCKS, b, t, IN_BLOCK_W)
    if past_k is None:
        oa = _attn_prompt(z4, prm["lam"], prm["a_subln"], lam_init)
        s0 = jnp.zeros((b, R_HEADS, HEAD_W, HEAD_W), F32)
    else:
        p = past_k.shape[1]
        oa = _attn_sample(z4, past_k.reshape(b, p, A_HEADS * HEAD_W), past_v.reshape(b, p, A_HEADS * HEAD_W),
                          prm["lam"], prm["a_subln"], lam_init)
        s0 = past_s.astype(F32)
    orr, s_new = _hgrn(z4, prm["r_lb_logits"], prm["r_gnorm"], s0)
    h1, xn2, q = _merge(oa.reshape(n, -1), orr.reshape(n, -1), z, x2,
                        prm["w_a"], prm["w_b"], prm["w_out"], prm["norm2"], prm["p_wq"])
    eidx, gate = _retrieve(q, prm["p_keys"])
    eidx = eidx.reshape(-1)
    chi8, clo8, chi, clo = _peer_down(eidx, xn2.reshape(n, ROW_TILE, 128), gate,
                                      prm["down_packed"], prm["sel"], prm["rep"])
    y = _peer_up(eidx, chi8, clo8, chi, clo, h1.reshape(n, ROW_TILE, 128), prm["final_norm"],
                 prm["up_packed"]).reshape(b, t, d)
    k_new = k_flat.reshape(1, b, t, A_HEADS, HEAD_W)
    v_new = v_flat.reshape(1, b, t, A_HEADS, HEAD_W)
    return y, k_new, v_new, s_new[None].astype(x.dtype)


def kernel(x_prompt, x_sample, cache_k, cache_v, state_hgrn, norm1, w_in, lam_params, a_subln, r_lb_logits, r_gnorm, w_a, w_b, w_out, norm2, p_wq, p_keys, p_down, p_up, final_norm):
    assert w_in.shape[0] == 1 and x_prompt.shape[-1] == 8 * 128
    sel = (lax.broadcasted_iota(jnp.int32, (HK * ROW_TILE, HK), 0) // ROW_TILE
           == lax.broadcasted_iota(jnp.int32, (HK * ROW_TILE, HK), 1)).astype(F32)
    prm = {
        "norm1": norm1[0][None].astype(F32),
        "w_in": w_in[0].astype(BF16),
        "lam": lam_params[0].astype(F32),
        "a_subln": a_subln[0][None].astype(F32),
        "r_lb_logits": r_lb_logits.astype(F32),
        "r_gnorm": r_gnorm[0][None].astype(F32),
        "w_a": w_a[0].astype(BF16),
        "w_b": w_b[0].astype(BF16),
        "w_out": w_out[0].astype(BF16),
        "norm2": norm2[0][None].astype(F32),
        "p_wq": p_wq[0].astype(BF16),
        "p_keys": p_keys[0].reshape(P_HEADS * 2, P_NKEYS, -1).astype(F32),
        "down_packed": _pack_table(p_down[0]),
        "up_packed": _pack_table(p_up[0]),
        "sel": sel,
        "rep": sel.T.astype(BF16),
        "final_norm": final_norm.reshape(ROW_TILE, 128).astype(F32),
    }
    y_p, k_p, v_p, s_p = _trunk(x_prompt, None, None, None, prm)
    y_s, k_s, v_s, s_s = _trunk(x_sample, cache_k[0], cache_v[0], state_hgrn[0], prm)
    return (y_p, y_s, k_p, v_p, s_p, k_s, v_s, s_s)
```

```python
import functools
import math

import jax
import jax.numpy as jnp
from jax import lax
from jax.experimental import pallas as pl
from jax.experimental.pallas import tpu as pltpu

F32 = jnp.float32
BF16 = jnp.bfloat16
EPS = 1e-6
CHUNK = 64
A_HEADS = 4
A_DK = 64
R_HEADS = 4
R_BLOCK = 16
P_HEADS = 8
P_NKEYS = 128
P_TOPK = 16
HK = P_HEADS * P_TOPK
HEAD_W = 128
N_IN_BLOCKS = 11
IN_BLOCK_W = 512
VMEM_LIMIT = 52 * 1024 * 1024


def _blk(n, pref):
    if n <= pref:
        return n
    b = pref
    while n % b:
        b //= 2
    assert b >= 8, (n, pref)
    return b


def _nt(a, b):
    return lax.dot_general(a, b, (((1,), (1,)), ((), ())), preferred_element_type=F32)


def _rms(x, gain):
    return x * lax.rsqrt(jnp.mean(x * x, axis=-1, keepdims=True) + EPS) * gain


K_BLOCK, V_BLOCK = 1, 2


def _inproj_kernel(x_ref, g_ref, w_ref, o_ref, k_ref, v_ref, xn_ref):
    j = pl.program_id(1)

    @pl.when(j == 0)
    def _():
        xn_ref[...] = _rms(x_ref[...], g_ref[...]).astype(BF16)

    r = jnp.dot(xn_ref[...], w_ref[...], preferred_element_type=F32)
    o_ref[...] = r

    @pl.when(j == K_BLOCK)
    def _():
        k_ref[...] = r

    @pl.when(j == V_BLOCK)
    def _():
        v_ref[...] = r


def _inproj(x, gain, w_bf16):
    n, d = x.shape
    tm = _blk(n, 2048)
    kv_spec = pl.BlockSpec((tm, IN_BLOCK_W), lambda i, j: (i, 0), pipeline_mode=pl.Buffered(1))
    return pl.pallas_call(
        _inproj_kernel,
        grid=(n // tm, N_IN_BLOCKS),
        in_specs=[
            pl.BlockSpec((tm, d), lambda i, j: (i, 0)),
            pl.BlockSpec((1, d), lambda i, j: (0, 0)),
            pl.BlockSpec((d, IN_BLOCK_W), lambda i, j: (0, j)),
        ],
        out_specs=[pl.BlockSpec((None, tm, IN_BLOCK_W), lambda i, j: (j, i, 0)), kv_spec, kv_spec],
        out_shape=[jax.ShapeDtypeStruct((N_IN_BLOCKS, n, IN_BLOCK_W), F32),
                   jax.ShapeDtypeStruct((n, IN_BLOCK_W), F32),
                   jax.ShapeDtypeStruct((n, IN_BLOCK_W), F32)],
        scratch_shapes=[pltpu.VMEM((tm, d), BF16)],
        compiler_params=pltpu.CompilerParams(
            dimension_semantics=("parallel", "arbitrary"), vmem_limit_bytes=VMEM_LIMIT),
        name="inproj",
    )(x, gain, w_bf16)


def _lam_from_params(lp, lam_init):
    a = jnp.sum(lp[0:1] * lp[1:2], axis=(0, 1), keepdims=True)
    b = jnp.sum(lp[2:3] * lp[3:4], axis=(0, 1), keepdims=True)
    return jnp.exp(a) - jnp.exp(b) + lam_init


def _map_masks():
    lane = lax.broadcasted_iota(jnp.int32, (1, HEAD_W), 1)
    m1 = (lane < A_DK).astype(F32)
    return m1, 1.0 - m1


def _chunk_id(pos):
    return lax.shift_right_logical(pos, int(math.log2(CHUNK)))


def _attn_prompt_kernel(lam_ref, q_ref, k_ref, v_ref, sub_ref, o_ref, *, seq, qb, lam_init):
    lam = _lam_from_params(lam_ref[...], lam_init)
    m1, m2 = _map_masks()
    kb = k_ref[...].astype(BF16)
    vb = v_ref[...].astype(BF16)
    gain = sub_ref[...] * (1.0 - lam_init)
    for j in range(seq // qb):
        kv_len = (j + 1) * qb
        q = q_ref[j * qb:(j + 1) * qb, :] * (A_DK ** -0.5)
        kk = kb[0:kv_len]
        s1 = _nt((q * m1).astype(BF16), kk)
        s2 = _nt((q * m2).astype(BF16), kk)
        qpos = j * qb + lax.broadcasted_iota(jnp.int32, (qb, kv_len), 0)
        kpos = lax.broadcasted_iota(jnp.int32, (qb, kv_len), 1)
        mask = _chunk_id(kpos) <= _chunk_id(qpos)
        s1 = jnp.where(mask, s1, -jnp.inf)
        s2 = jnp.where(mask, s2, -jnp.inf)
        e1 = jnp.exp(s1 - jnp.max(s1, axis=-1, keepdims=True))
        e2 = jnp.exp(s2 - jnp.max(s2, axis=-1, keepdims=True))
        r1 = 1.0 / jnp.sum(e1, axis=-1, keepdims=True)
        r2 = lam / jnp.sum(e2, axis=-1, keepdims=True)
        w = (e1 * r1 - e2 * r2).astype(BF16)
        o = jnp.dot(w, vb[0:kv_len], preferred_element_type=F32)
        o_ref[j * qb:(j + 1) * qb, :] = _rms(o, gain)


def _attn_prompt(z4, lam_params, a_subln, lam_init):
    _, b, t, _ = z4.shape
    qb = _blk(t, 256)

    def zspec(col):
        return pl.BlockSpec((None, None, t, HEAD_W), lambda bi, h, col=col: (col, bi, 0, h))

    return pl.pallas_call(
        functools.partial(_attn_prompt_kernel, seq=t, qb=qb, lam_init=lam_init),
        grid=(b, A_HEADS),
        in_specs=[
            pl.BlockSpec((4, A_DK), lambda bi, h: (0, 0)),
            zspec(0), zspec(1), zspec(2),
            pl.BlockSpec((1, HEAD_W), lambda bi, h: (0, 0)),
        ],
        out_specs=pl.BlockSpec((None, t, HEAD_W), lambda bi, h: (bi, 0, h)),
        out_shape=jax.ShapeDtypeStruct((b, t, A_HEADS * HEAD_W), F32),
        compiler_params=pltpu.CompilerParams(
            dimension_semantics=("parallel", "parallel"), vmem_limit_bytes=VMEM_LIMIT),
        name="attn_prompt",
    )(lam_params, z4, z4, z4, a_subln)


def _attn_sample_kernel(lam_ref, q_ref, k_ref, v_ref, pk_ref, pv_ref, sub_ref, o_ref, *, seq, past, lam_init):
    lam = _lam_from_params(lam_ref[...], lam_init)
    m1, m2 = _map_masks()
    gain = sub_ref[...] * (1.0 - lam_init)
    q = q_ref[...] * (A_DK ** -0.5)
    kn = k_ref[...].astype(BF16)
    kp = pk_ref[...].astype(BF16)
    qpos_p = past + lax.broadcasted_iota(jnp.int32, (seq, past), 0)
    kpos_p = lax.broadcasted_iota(jnp.int32, (seq, past), 1)
    mask_p = _chunk_id(kpos_p) <= _chunk_id(qpos_p)
    qpos_n = past + lax.broadcasted_iota(jnp.int32, (seq, seq), 0)
    kpos_n = past + lax.broadcasted_iota(jnp.int32, (seq, seq), 1)
    mask_n = _chunk_id(kpos_n) <= _chunk_id(qpos_n)

    def one_map(qm):
        sp = jnp.where(mask_p, _nt(qm, kp), -jnp.inf)
        sn = jnp.where(mask_n, _nt(qm, kn), -jnp.inf)
        mx = jnp.maximum(jnp.max(sp, axis=-1, keepdims=True), jnp.max(sn, axis=-1, keepdims=True))
        ep = jnp.exp(sp - mx)
        en = jnp.exp(sn - mx)
        tot = jnp.sum(ep, axis=-1, keepdims=True) + jnp.sum(en, axis=-1, keepdims=True)
        return ep, en, tot

    ep1, en1, t1 = one_map((q * m1).astype(BF16))
    ep2, en2, t2 = one_map((q * m2).astype(BF16))
    r1 = 1.0 / t1
    r2 = lam / t2
    wp = (ep1 * r1 - ep2 * r2).astype(BF16)
    wn = (en1 * r1 - en2 * r2).astype(BF16)
    o = (jnp.dot(wp, pv_ref[...].astype(BF16), preferred_element_type=F32)
         + jnp.dot(wn, v_ref[...].astype(BF16), preferred_element_type=F32))
    o_ref[...] = _rms(o, gain)


def _attn_sample(z4, past_k, past_v, lam_params, a_subln, lam_init):
    _, b, t, _ = z4.shape
    p = past_k.shape[1]

    def zspec(col):
        return pl.BlockSpec((None, None, t, HEAD_W), lambda bi, h, col=col: (col, bi, 0, h))

    pspec = pl.BlockSpec((None, p, HEAD_W), lambda bi, h: (bi, 0, h))
    return pl.pallas_call(
        functools.partial(_attn_sample_kernel, seq=t, past=p, lam_init=lam_init),
        grid=(b, A_HEADS),
        in_specs=[
            pl.BlockSpec((4, A_DK), lambda bi, h: (0, 0)),
            zspec(0), zspec(1), zspec(2), pspec, pspec,
            pl.BlockSpec((1, HEAD_W), lambda bi, h: (0, 0)),
        ],
        out_specs=pl.BlockSpec((None, t, HEAD_W), lambda bi, h: (bi, 0, h)),
        out_shape=jax.ShapeDtypeStruct((b, t, A_HEADS * HEAD_W), F32),
        compiler_params=pltpu.CompilerParams(
            dimension_semantics=("parallel", "parallel"), vmem_limit_bytes=VMEM_LIMIT),
        name="attn_sample",
    )(lam_params, z4, z4, z4, past_k, past_v, a_subln)


R_HEADS_PER_STEP = 2


def _hgrn_kernel(f_ref, q_ref, i_ref, og_ref, lbl_ref, gn_ref, s0_ref, o_ref, s_ref, st_ref, *, seq, ch):
    nb = ch // R_BLOCK
    shift = int(math.log2(R_BLOCK))
    logits = lbl_ref[...]
    ex = jnp.exp(logits - jnp.max(logits, axis=0, keepdims=True))
    lb_all = ex[0:1] / jnp.sum(ex, axis=0, keepdims=True)
    row = lax.broadcasted_iota(jnp.int32, (ch, ch), 0)
    col = lax.broadcasted_iota(jnp.int32, (ch, ch), 1)
    same = lax.shift_right_logical(row, shift) == lax.shift_right_logical(col, shift)
    causal = jnp.logical_and(same, col <= row)
    sum_m = jnp.concatenate([causal, same], axis=0).astype(BF16)
    gn = gn_ref[...]
    for hh in range(R_HEADS_PER_STEP):
        st_ref[hh] = s0_ref[hh].T

    def one_head(hh, sl):
        cs = slice(hh * HEAD_W, (hh + 1) * HEAD_W)
        lb = lb_all[:, cs]
        g = lb + (1.0 - lb) * jax.nn.sigmoid(f_ref[sl, cs])
        logf = jnp.log(g)
        kk = 1.0 - g
        qp = q_ref[sl, cs]
        qq = qp * jax.nn.sigmoid(qp)
        vv = i_ref[sl, cs]
        p0 = logf.astype(BF16)
        r0 = logf - p0.astype(F32)
        p1 = r0.astype(BF16)
        p2 = (r0 - p1.astype(F32)).astype(BF16)
        sums = jnp.dot(sum_m, jnp.concatenate([p0, p1, p2], axis=1), preferred_element_type=F32)
        sums = sums[:, 0:HEAD_W] + sums[:, HEAD_W:2 * HEAD_W] + sums[:, 2 * HEAD_W:]
        b = sums[0:ch]
        bl = sums[ch:]
        q_in = (qq * jnp.exp(b)).astype(BF16)
        k_in = (kk * jnp.exp(-b)).astype(BF16)
        k_out = (kk * jnp.exp(bl - b)).astype(BF16)
        vb = vv.astype(BF16)
        a = jnp.where(causal, _nt(q_in, k_in), 0.0)
        o_intra = jnp.dot(a.astype(BF16), vb, preferred_element_type=F32)
        upds = [lax.dot_general(vb[r0:r0 + R_BLOCK], k_out[r0:r0 + R_BLOCK],
                                (((0,), (0,)), ((), ())), preferred_element_type=F32)
                for r0 in range(0, ch, R_BLOCK)]
        sts = [st_ref[hh]]
        for blk in range(nb):
            dl = jnp.exp(bl[blk * R_BLOCK:blk * R_BLOCK + 1])
            sts.append(sts[-1] * dl + upds[blk])
        st_ref[hh] = sts[nb]
        inter = [_nt(q_in[blk * R_BLOCK:(blk + 1) * R_BLOCK], sts[blk].astype(BF16))
                 for blk in range(nb)]
        o = o_intra + jnp.concatenate(inter, axis=0)
        ogp = og_ref[sl, cs]
        o_ref[sl, cs] = _rms(o, gn) * (ogp * jax.nn.sigmoid(ogp))

    def chunk(c, carry):
        sl = pl.ds(pl.multiple_of(c * ch, ch), ch)
        for hh in range(R_HEADS_PER_STEP):
            one_head(hh, sl)
        return carry

    lax.fori_loop(0, seq // ch, chunk, 0)
    for hh in range(R_HEADS_PER_STEP):
        s_ref[hh] = st_ref[hh].T


def _hgrn(z4, r_lb_logits, r_gnorm, s0):
    _, b, t, _ = z4.shape
    assert t % R_BLOCK == 0 and R_HEADS % R_HEADS_PER_STEP == 0
    ch = _blk(t, 128)
    nl = r_lb_logits.shape[0]
    hw = R_HEADS_PER_STEP * HEAD_W

    def zspec(col):
        return pl.BlockSpec((None, None, t, hw), lambda bi, h, col=col: (col, bi, 0, h))

    sspec = pl.BlockSpec((None, R_HEADS_PER_STEP, HEAD_W, HEAD_W), lambda bi, h: (bi, h, 0, 0))
    return pl.pallas_call(
        functools.partial(_hgrn_kernel, seq=t, ch=ch),
        grid=(b, R_HEADS // R_HEADS_PER_STEP),
        in_specs=[
            zspec(3), zspec(4), zspec(5), zspec(6),
            pl.BlockSpec((nl, hw), lambda bi, h: (0, h)),
            pl.BlockSpec((1, HEAD_W), lambda bi, h: (0, 0)),
            sspec,
        ],
        out_specs=[pl.BlockSpec((None, t, hw), lambda bi, h: (bi, 0, h)), sspec],
        out_shape=[jax.ShapeDtypeStruct((b, t, R_HEADS * HEAD_W), F32),
                   jax.ShapeDtypeStruct((b, R_HEADS, HEAD_W, HEAD_W), F32)],
        scratch_shapes=[pltpu.VMEM((R_HEADS_PER_STEP, HEAD_W, HEAD_W), F32)],
        compiler_params=pltpu.CompilerParams(
            dimension_semantics=("parallel", "parallel"), vmem_limit_bytes=VMEM_LIMIT),
        name="hgrn2",
    )(z4, z4, z4, z4, r_lb_logits, r_gnorm, s0)


def _merge_kernel(oa_ref, or_ref, ga0_ref, ga1_ref, gb0_ref, gb1_ref, x_ref,
                  wa_ref, wb_ref, wo_ref, n2_ref, wq_ref, h_ref, xn_ref, q_ref):
    pa = jnp.dot(oa_ref[...].astype(BF16), wa_ref[...], preferred_element_type=F32)
    pb = jnp.dot(or_ref[...].astype(BF16), wb_ref[...], preferred_element_type=F32)
    ga = jnp.concatenate([ga0_ref[...], ga1_ref[...]], axis=-1)
    gb = jnp.concatenate([gb0_ref[...], gb1_ref[...]], axis=-1)
    m = jax.nn.sigmoid(ga) * pa + jax.nn.sigmoid(gb) * pb
    h = x_ref[...] + jnp.dot(m.astype(BF16), wo_ref[...], preferred_element_type=F32)
    h_ref[...] = h
    xn = _rms(h, n2_ref[...])
    xn_ref[...] = xn
    q_ref[...] = jnp.dot(xn.astype(BF16), wq_ref[...], preferred_element_type=F32).astype(BF16)


def _merge(oa, orr, z, x, w_a, w_b, w_out, norm2, p_wq):
    n, d = x.shape
    tm = _blk(n, 512)
    qw = p_wq.shape[1]

    def zspec(col):
        return pl.BlockSpec((None, tm, IN_BLOCK_W), lambda i, col=col: (col, i, 0))

    def full(a):
        return pl.BlockSpec(a.shape, lambda i: (0,) * a.ndim, pipeline_mode=pl.Buffered(1))

    return pl.pallas_call(
        _merge_kernel,
        grid=(n // tm,),
        in_specs=[
            pl.BlockSpec((tm, oa.shape[1]), lambda i: (i, 0)),
            pl.BlockSpec((tm, orr.shape[1]), lambda i: (i, 0)),
            zspec(7), zspec(8), zspec(9), zspec(10),
            pl.BlockSpec((tm, d), lambda i: (i, 0)),
            full(w_a), full(w_b), full(w_out), full(norm2), full(p_wq),
        ],
        out_specs=[pl.BlockSpec((tm, d), lambda i: (i, 0)),
                   pl.BlockSpec((tm, d), lambda i: (i, 0)),
                   pl.BlockSpec((tm, qw), lambda i: (i, 0))],
        out_shape=[jax.ShapeDtypeStruct((n, d), F32),
                   jax.ShapeDtypeStruct((n, d), F32),
                   jax.ShapeDtypeStruct((n, qw), BF16)],
        compiler_params=pltpu.CompilerParams(
            dimension_semantics=("parallel",), vmem_limit_bytes=VMEM_LIMIT),
        name="merge_proj",
    )(oa, orr, z, z, z, z, x, w_a, w_b, w_out, norm2, p_wq)


TIE_SHIFT = 1024.0


def _top16(s, order, sentinel):
    vals, picks = [], []
    for _ in range(P_TOPK):
        m = jnp.max(s, axis=0, keepdims=True)
        am = jnp.min(jnp.where(s == m, order, sentinel), axis=0, keepdims=True)
        vals.append(m)
        picks.append(am)
        s = jnp.where(order == am, -jnp.inf, s)
    return vals, picks, None


def _top16_no_ties(s, order, sentinel):
    del sentinel
    vals, keys = [], []
    for _ in range(P_TOPK):
        m = jnp.max(s, axis=0, keepdims=True)
        eq = s == m
        keys.append(jnp.sum(jnp.where(eq, order + TIE_SHIFT, 0.0), axis=0, keepdims=True))
        vals.append(m)
        s = jnp.where(eq, -jnp.inf, s)
    worst = functools.reduce(jnp.maximum, keys)
    return vals, [k - TIE_SHIFT for k in keys], worst >= 2.0 * TIE_SHIFT


def _pair_candidates(tt):
    half = P_TOPK // 2
    j16 = lax.broadcasted_iota(jnp.int32, (P_TOPK, tt), 0)
    j8 = lax.broadcasted_iota(jnp.int32, (half, tt), 0)
    pieces = [j16] + [i * P_TOPK + j8 for i in range(1, half)] + [(half + j8) * P_TOPK]
    return jnp.concatenate(pieces, axis=0).astype(F32)


def _retrieve_block(q_ref, keys_ref, tt, top16):
    half = P_TOPK // 2
    iota16 = lax.broadcasted_iota(jnp.int32, (P_TOPK, tt), 0).astype(F32)
    key_iota = lax.broadcasted_iota(jnp.int32, (P_NKEYS, tt), 0).astype(F32)
    flat = _pair_candidates(tt)
    e_rows, g_rows, flags = [], [], []
    for h in range(P_HEADS):
        sv, si = [], []
        for c in range(2):
            hc = h * 2 + c
            s = _nt(keys_ref[hc], q_ref[:, hc * P_NKEYS:(hc + 1) * P_NKEYS])
            vals, idxs, flag = top16(s, key_iota, float(P_NKEYS))
            flags.append(flag)
            sv.append(jnp.concatenate(vals, axis=0))
            si.append(jnp.concatenate(idxs, axis=0))
        comb = jnp.concatenate(
            [sv[0][0:1] + sv[1]]
            + [sv[0][i:i + 1] + sv[1][0:half] for i in range(1, half)]
            + [sv[0][half:] + sv[1][0:1]], axis=0)
        cvals, cidx, flag = top16(comb, flat, float(P_TOPK * P_TOPK))
        flags.append(flag)
        for k in range(P_TOPK):
            i0 = jnp.floor(cidx[k] * (1.0 / P_TOPK))
            i1 = cidx[k] - i0 * P_TOPK
            e0 = jnp.sum(jnp.where(iota16 == i0, si[0], 0.0), axis=0, keepdims=True)
            e1 = jnp.sum(jnp.where(iota16 == i1, si[1], 0.0), axis=0, keepdims=True)
            e_rows.append(e0 * P_NKEYS + e1)
        cv = jnp.concatenate(cvals, axis=0)
        ex = jnp.exp(cv - cvals[0])
        g_rows.append(ex / jnp.sum(ex, axis=0, keepdims=True))
    e_all = (jnp.concatenate(e_rows, axis=0) * 4.0).astype(jnp.int32)
    g_all = jnp.concatenate(g_rows, axis=0)
    flags = [f for f in flags if f is not None]
    return e_all, g_all, (functools.reduce(jnp.logical_or, flags) if flags else None)


def _retrieve_kernel(q_ref, keys_ref, e_ref, g_ref, *, tt):
    e_all, g_all, tie = _retrieve_block(q_ref, keys_ref, tt, _top16_no_ties)
    e_ref[...] = e_all
    g_ref[...] = g_all.T

    @pl.when(jnp.max(jnp.where(tie, 1.0, 0.0)) > 0.0)
    def _():
        e_x, g_x, _ = _retrieve_block(q_ref, keys_ref, tt, _top16)
        e_ref[...] = e_x
        g_ref[...] = g_x.T


def _retrieve(q, keys16):
    n, qw = q.shape
    tt = _blk(n, 128)
    return pl.pallas_call(
        functools.partial(_retrieve_kernel, tt=tt),
        grid=(n // tt,),
        in_specs=[pl.BlockSpec((tt, qw), lambda i: (i, 0)),
                  pl.BlockSpec(keys16.shape, lambda i: (0, 0, 0))],
        out_specs=[pl.BlockSpec((HK, tt), lambda i: (i, 0)),
                   pl.BlockSpec((tt, HK), lambda i: (i, 0))],
        out_shape=[jax.ShapeDtypeStruct((n // tt * HK, tt), jnp.int32),
                   jax.ShapeDtypeStruct((n, HK), F32)],
        compiler_params=pltpu.CompilerParams(
            dimension_semantics=("parallel",), vmem_limit_bytes=VMEM_LIMIT),
        name="peer_retrieve",
    )(q, keys16)


ROW_TILE = 8
GATHER_GROUP = 32
UP_VPU_PAIRS = 32


def _pack_table(tbl):
    e = tbl.shape[0]
    u = lax.bitcast_convert_type(tbl.astype(BF16), jnp.uint16).astype(jnp.uint32).reshape(e, 4, 2, 128)
    w = u[:, :, 0, :] | (u[:, :, 1, :] << 16)
    return lax.bitcast_convert_type(w, jnp.int32).reshape(e * 4, 128)


def _expert_row(tbl_ref, e):
    return tbl_ref[pl.ds(pl.multiple_of(e, 4), 4), :]


def _gather_rows(read_idx, tbl_ref, slab_ref, t, first, count):
    for m in range(0, count, 2):
        pair = [_expert_row(tbl_ref, read_idx(t, first + m + k)) for k in range(2)]
        slab_ref[m * 4:(m + 2) * 4, :] = jnp.concatenate(pair, axis=0)


def _with_index_block(idx_hbm, bufs, sems, tb, body):
    i = pl.program_id(0)
    n_steps = pl.num_programs(0)
    blk = HK * tb

    def copy(step, slot):
        return pltpu.make_async_copy(idx_hbm.at[pl.ds(pl.multiple_of(step * blk, blk), blk)], bufs[slot], sems.at[slot])

    @pl.when(i == 0)
    def _():
        copy(0, 0).start()

    for slot in range(2):
        @pl.when(i % 2 == slot)
        def _(slot=slot):
            copy(i, slot).wait()

            @pl.when(i + 1 < n_steps)
            def _():
                copy(i + 1, 1 - slot).start()

            buf = bufs[slot]
            body(lambda t, hk: buf.at[pl.ds(hk * tb, tb)][t])


def _split_bf16(x):
    hi = x.astype(BF16)
    lo = (x - hi.astype(F32)).astype(BF16)
    return hi, lo


def _diag_mask(rows, pairs):
    r = lax.broadcasted_iota(jnp.int32, (rows, pairs * ROW_TILE), 0)
    c = lax.broadcasted_iota(jnp.int32, (rows, pairs * ROW_TILE), 1)
    return jnp.bitwise_and(c, ROW_TILE - 1) == jnp.bitwise_and(r, ROW_TILE - 1)


def _pipelined_tokens(slabs, gather, tail, tb):
    group = min(GATHER_GROUP, tb)
    slabs[1][...] = jnp.zeros_like(slabs[1])

    def body(i, carry):
        t0 = group * i
        for j in range(group):
            gather(slabs[j % 2], t0 + j)
            tail(slabs[(j + 1) % 2], jnp.maximum(t0 + j - 1, 0))
        return carry

    lax.fori_loop(0, tb // group, body, 0)
    tail(slabs[1], tb - 1)


def _down_kernel(idx_hbm, x_ref, gate_ref, tbl_ref, sel_ref, rep_ref, chi8_ref, clo8_ref, chi_ref, clo_ref,
                 slab_a, slab_b, rows_ref, idx_a, idx_b, sems, *, tb):
    mask = _diag_mask(ROW_TILE, HK)

    def tail(slab_ref, t):
        hi, lo = _split_bf16(x_ref[t])
        g = _nt(jnp.concatenate([hi, lo], axis=0), pltpu.bitcast(slab_ref[...], BF16))
        gm = jnp.where(mask, g[0:ROW_TILE] + g[ROW_TILE:], 0.0)
        rows_ref[pl.ds(t, 1), :] = jnp.sum(gm, axis=0, keepdims=True)

    def body(read_idx):
        def gather(slab_ref, t):
            _gather_rows(read_idx, tbl_ref, slab_ref, t, 0, HK)

        _pipelined_tokens((slab_a, slab_b), gather, tail, tb)

    _with_index_block(idx_hbm, (idx_a, idx_b), sems, tb, body)
    hid = jnp.dot(rows_ref[...], sel_ref[...], precision=lax.Precision.HIGHEST, preferred_element_type=F32)
    c = gate_ref[...] * (0.5 * hid * (1.0 + lax.erf(hid * (2.0 ** -0.5))))
    chi, clo = _split_bf16(c)
    chi_ref[...] = chi.astype(F32)
    clo_ref[...] = clo.astype(F32)
    chi8_ref[...] = jnp.dot(chi, rep_ref[...], preferred_element_type=F32)
    clo8_ref[...] = jnp.dot(clo, rep_ref[...], preferred_element_type=F32)


def _peer_down(idx, x3, gate, tbl_packed, sel, rep):
    n = x3.shape[0]
    tb = _blk(n, 128)
    assert tb % 2 == 0
    wide = HK * ROW_TILE
    row_spec = pl.BlockSpec((tb, HK), lambda i: (i, 0))
    wide_spec = pl.BlockSpec((tb, wide), lambda i: (i, 0))
    return pl.pallas_call(
        functools.partial(_down_kernel, tb=tb),
        grid=(n // tb,),
        in_specs=[
            pl.BlockSpec(memory_space=pl.ANY),
            pl.BlockSpec((tb, ROW_TILE, 128), lambda i: (i, 0, 0)),
            row_spec,
            pl.BlockSpec(memory_space=pltpu.VMEM),
            pl.BlockSpec(memory_space=pltpu.VMEM),
            pl.BlockSpec(memory_space=pltpu.VMEM),
        ],
        out_specs=[wide_spec, wide_spec, row_spec, row_spec],
        out_shape=[jax.ShapeDtypeStruct((n, wide), F32), jax.ShapeDtypeStruct((n, wide), F32),
                   jax.ShapeDtypeStruct((n, HK), F32), jax.ShapeDtypeStruct((n, HK), F32)],
        scratch_shapes=[pltpu.VMEM((HK * 4, 128), jnp.int32), pltpu.VMEM((HK * 4, 128), jnp.int32),
                        pltpu.VMEM((tb, wide), F32),
                        pltpu.SMEM((HK * tb,), jnp.int32), pltpu.SMEM((HK * tb,), jnp.int32),
                        pltpu.SemaphoreType.DMA((2,))],
        compiler_params=pltpu.CompilerParams(
            dimension_semantics=("arbitrary",), vmem_limit_bytes=VMEM_LIMIT),
        name="peer_down",
    )(idx, x3, gate, tbl_packed, sel, rep)


def _up_kernel(idx_hbm, chi8_ref, clo8_ref, chi_ref, clo_ref, h_ref, g_ref, tbl_ref, out_ref,
               slab_a, slab_b, splat_ref, idx_a, idx_b, sems, *, tb):
    nv = UP_VPU_PAIRS
    nm = HK - nv
    wide = nm * ROW_TILE
    mask = _diag_mask(2 * ROW_TILE, nm)
    eye = (lax.broadcasted_iota(jnp.int32, (nv, HK), 0) == lax.broadcasted_iota(jnp.int32, (nv, HK), 1))
    ones = jnp.ones((2 * HK, 128), BF16)

    def vpu_pairs(read_idx, t):
        dh = jnp.where(eye, chi_ref[pl.ds(t, 1), :], 0.0)
        dl = jnp.where(eye, clo_ref[pl.ds(t, 1), :], 0.0)
        splat_ref[...] = jnp.dot(jnp.concatenate([dh, dl], axis=1).astype(BF16), ones, preferred_element_type=F32)
        accs = [jnp.zeros((4, 128), F32) for _ in range(4)]
        for hk in range(nv):
            w = _expert_row(tbl_ref, read_idx(t, hk))
            even = lax.bitcast_convert_type(w << 16, F32)
            odd = lax.bitcast_convert_type(w & jnp.int32(-65536), F32)
            cs = jnp.broadcast_to(splat_ref[hk:hk + 1, :], (4, 128))
            k = (hk % 2) * 2
            accs[k] = accs[k] + cs * even
            accs[k + 1] = accs[k + 1] + cs * odd
        out_ref[t, pl.ds(0, 4, stride=2), :] = accs[0] + accs[2]
        out_ref[t, pl.ds(1, 4, stride=2), :] = accs[1] + accs[3]

    def tail(slab_ref, t):
        hi = jnp.broadcast_to(chi8_ref[pl.ds(t, 1), nv * ROW_TILE:], (ROW_TILE, wide))
        lo = jnp.broadcast_to(clo8_ref[pl.ds(t, 1), nv * ROW_TILE:], (ROW_TILE, wide))
        lhs = jnp.where(mask, jnp.concatenate([hi, lo], axis=0), 0.0).astype(BF16)
        o = jnp.dot(lhs, pltpu.bitcast(slab_ref[...], BF16), preferred_element_type=F32)
        out_ref[t] = out_ref[t] + (o[0:ROW_TILE] + o[ROW_TILE:])

    def body(read_idx):
        def gather(slab_ref, t):
            _gather_rows(read_idx, tbl_ref, slab_ref, t, nv, nm)
            vpu_pairs(read_idx, t)

        _pipelined_tokens((slab_a, slab_b), gather, tail, tb)

    _with_index_block(idx_hbm, (idx_a, idx_b), sems, tb, body)
    v = out_ref[...] + h_ref[...]
    ms = jnp.sum(jnp.sum(v * v, axis=2, keepdims=True), axis=1, keepdims=True) * (1.0 / (ROW_TILE * 128))
    out_ref[...] = v * lax.rsqrt(ms + EPS) * g_ref[...]


def _peer_up(idx, chi8, clo8, chi, clo, h3, gain_tile, tbl_packed):
    n = h3.shape[0]
    tb = _blk(n, 128)
    assert tb % 2 == 0 and UP_VPU_PAIRS % ROW_TILE == 0
    nm = HK - UP_VPU_PAIRS
    row_spec = pl.BlockSpec((tb, HK), lambda i: (i, 0))
    wide_spec = pl.BlockSpec((tb, HK * ROW_TILE), lambda i: (i, 0))
    return pl.pallas_call(
        functools.partial(_up_kernel, tb=tb),
        grid=(n // tb,),
        in_specs=[
            pl.BlockSpec(memory_space=pl.ANY),
            wide_spec, wide_spec, row_spec, row_spec,
            pl.BlockSpec((tb, ROW_TILE, 128), lambda i: (i, 0, 0)),
            pl.BlockSpec((ROW_TILE, 128), lambda i: (0, 0)),
            pl.BlockSpec(memory_space=pltpu.VMEM),
        ],
        out_specs=pl.BlockSpec((tb, ROW_TILE, 128), lambda i: (i, 0, 0)),
        out_shape=jax.ShapeDtypeStruct((n, ROW_TILE, 128), F32),
        scratch_shapes=[pltpu.VMEM((nm * 4, 128), jnp.int32), pltpu.VMEM((nm * 4, 128), jnp.int32),
                        pltpu.VMEM((UP_VPU_PAIRS, 128), F32),
                        pltpu.SMEM((HK * tb,), jnp.int32), pltpu.SMEM((HK * tb,), jnp.int32),
                        pltpu.SemaphoreType.DMA((2,))],
        compiler_params=pltpu.CompilerParams(
            dimension_semantics=("arbitrary",), vmem_limit_bytes=VMEM_LIMIT),
        name="peer_up",
    )(idx, chi8, clo8, chi, clo, h3, gain_tile, tbl_packed)


def _trunk(x, past_k, past_v, past_s, prm):
    b, t, d = x.shape
    n = b * t
    lam_init = 0.8 - 0.6 * math.exp(-0.3 * 0)
    x2 = x.reshape(n, d)
    z, k_flat, v_flat = _inproj(x2, prm["norm1"], prm["w_in"])
    z4 = z.reshape(N_IN_BLOCKS, b, t, IN_BLOCK_W)
    if past_k is None:
        oa = _attn_prompt(z4, prm["lam"], prm["a_subln"], lam_init)
        s0 = jnp.zeros((b, R_HEADS, HEAD_W, HEAD_W), F32)
    else:
        p = past_k.shape[1]
        oa = _attn_sample(z4, past_k.reshape(b, p, A_HEADS * HEAD_W), past_v.reshape(b, p, A_HEADS * HEAD_W),
                          prm["lam"], prm["a_subln"], lam_init)
        s0 = past_s.astype(F32)
    orr, s_new = _hgrn(z4, prm["r_lb_logits"], prm["r_gnorm"], s0)
    h1, xn2, q = _merge(oa.reshape(n, -1), orr.reshape(n, -1), z, x2,
                        prm["w_a"], prm["w_b"], prm["w_out"], prm["norm2"], prm["p_wq"])
    eidx, gate = _retrieve(q, prm["p_keys"])
    eidx = eidx.reshape(-1)
    chi8, clo8, chi, clo = _peer_down(eidx, xn2.reshape(n, ROW_TILE, 128), gate,
                                      prm["down_packed"], prm["sel"], prm["rep"])
    y = _peer_up(eidx, chi8, clo8, chi, clo, h1.reshape(n, ROW_TILE, 128), prm["final_norm"],
                 prm["up_packed"]).reshape(b, t, d)
    k_new = k_flat.reshape(1, b, t, A_HEADS, HEAD_W)
    v_new = v_flat.reshape(1, b, t, A_HEADS, HEAD_W)
    return y, k_new, v_new, s_new[None].astype(x.dtype)


def kernel(x_prompt, x_sample, cache_k, cache_v, state_hgrn, norm1, w_in, lam_params, a_subln, r_lb_logits, r_gnorm, w_a, w_b, w_out, norm2, p_wq, p_keys, p_down, p_up, final_norm):
    assert w_in.shape[0] == 1 and x_prompt.shape[-1] == 8 * 128
    sel = (lax.broadcasted_iota(jnp.int32, (HK * ROW_TILE, HK), 0) // ROW_TILE
           == lax.broadcasted_iota(jnp.int32, (HK * ROW_TILE, HK), 1)).astype(F32)
    prm = {
        "norm1": norm1[0][None].astype(F32),
        "w_in": w_in[0].astype(BF16),
        "lam": lam_params[0].astype(F32),
        "a_subln": a_subln[0][None].astype(F32),
        "r_lb_logits": r_lb_logits.astype(F32),
        "r_gnorm": r_gnorm[0][None].astype(F32),
        "w_a": w_a[0].astype(BF16),
        "w_b": w_b[0].astype(BF16),
        "w_out": w_out[0].astype(BF16),
        "norm2": norm2[0][None].astype(F32),
        "p_wq": p_wq[0].astype(BF16),
        "p_keys": p_keys[0].reshape(P_HEADS * 2, P_NKEYS, -1).astype(BF16),
        "down_packed": _pack_table(p_down[0]),
        "up_packed": _pack_table(p_up[0]),
        "sel": sel,
        "rep": sel.T.astype(BF16),
        "final_norm": final_norm.reshape(ROW_TILE, 128).astype(F32),
    }
    y_p, k_p, v_p, s_p = _trunk(x_prompt, None, None, None, prm)
    y_s, k_s, v_s, s_s = _trunk(x_sample, cache_k[0], cache_v[0], state_hgrn[0], prm)
    return (y_p, y_s, k_p, v_p, s_p, k_s, v_s, s_s)
```
